```python
import math
import jax, jax.numpy as jnp
from jax import lax
import numpy as np

D_MODEL = 4096
BATCH = 4
SEQ = 2048
DEPTH = 4
DEC_BATCH = 128
DEC_SEQ = 1
PAST_LEN = 16384
PAGE_SIZE = 128

N_DELTA_HEADS = 16
DELTA_HEAD_DIM = 128
DELTA_WIDTH = N_DELTA_HEADS * DELTA_HEAD_DIM
CONV_WIDTH = 4
DELTA_CHUNK = 64
S5_WIDTH = D_MODEL - DELTA_WIDTH
S5_GROUP = 16
N_S5_GROUPS = S5_WIDTH // S5_GROUP
S5_STATE = 64
S5_BLOCK = 128
IN_WIDTH = 4 * DELTA_WIDTH + 2 * N_DELTA_HEADS + S5_WIDTH
D_FF = -(-8 * D_MODEL // (3 * 256)) * 256
EPS = 1e-6

kernel_name = 'hymba_gated_deltanet_s5_decode_step'


def rmsnorm(x, w):
    xf = x.astype(jnp.float32)
    y = xf * lax.rsqrt(jnp.mean(xf * xf, axis=-1, keepdims=True) + EPS)
    return (y * w.astype(jnp.float32)).astype(x.dtype)


def l2norm(x):
    xf = x.astype(jnp.float32)
    return xf * lax.rsqrt(jnp.sum(xf * xf, axis=-1, keepdims=True) + EPS)


def causal_conv(x, buf, w):
    L = x.shape[1]
    xpad = jnp.concatenate([buf.astype(x.dtype), x], axis=1)
    out = sum(xpad[:, i:i + L] * w[i] for i in range(CONV_WIDTH))
    return out, xpad[:, L:]


def gated_delta_rule(q, k, v, g, beta, S0):
    B, L, H, dk = q.shape
    dv = v.shape[-1]
    C = DELTA_CHUNK if L % DELTA_CHUNK == 0 else L
    n = L // C

    def chunks(t):
        return jnp.moveaxis(t.reshape((B, n, C) + t.shape[2:]), 3, 2)

    q, k, v, g, beta = chunks(q), chunks(k), chunks(v), chunks(g), chunks(beta)
    gc = jnp.cumsum(g, axis=-1)
    diff = gc[..., :, None] - gc[..., None, :]
    tri_incl = jnp.tril(jnp.ones((C, C), dtype=bool))
    tri_strict = jnp.tril(jnp.ones((C, C), dtype=bool), -1)
    decay_incl = jnp.exp(jnp.where(tri_incl, diff, -jnp.inf))
    decay_strict = jnp.where(tri_strict, decay_incl, 0.0)
    kb = k * beta[..., None]
    vb = v * beta[..., None]
    a_mat = jnp.eye(C, dtype=jnp.float32) + jnp.einsum('bnhid,bnhjd->bnhij', kb, k) * decay_strict
    rhs = jnp.concatenate([vb, kb * jnp.exp(gc)[..., None]], axis=-1)
    sol = lax.linalg.triangular_solve(a_mat, rhs, left_side=True, lower=True)
    value, k_cum = sol[..., :dv], sol[..., dv:]
    attn = jnp.einsum('bnhid,bnhjd->bnhij', q, k) * decay_incl
    q_dec = q * jnp.exp(gc)[..., None]
    k_dec = k * jnp.exp(gc[..., -1:] - gc)[..., None]
    g_last = jnp.exp(gc[..., -1])

    def step(S, xs):
        value_c, kcum_c, attn_c, qdec_c, kdec_c, glast_c = xs
        v_new = value_c - jnp.einsum('bhcd,bhde->bhce', kcum_c, S)
        o = jnp.einsum('bhcd,bhde->bhce', qdec_c, S) + jnp.einsum('bhij,bhje->bhie', attn_c, v_new)
        S = S * glast_c[..., None, None] + jnp.einsum('bhcd,bhce->bhde', kdec_c, v_new)
        return S, o

    xs = tuple(jnp.moveaxis(t, 1, 0) for t in (value, k_cum, attn, q_dec, k_dec, g_last))
    S_final, o = lax.scan(step, S0, xs)
    o = o.transpose(1, 0, 3, 2, 4).reshape(B, L, H, dv)
    return o, S_final


def s5_combine(e1, e2):
    a1r, a1i, b1r, b1i = e1
    a2r, a2i, b2r, b2i = e2
    ar = a2r * a1r - a2i * a1i
    ai = a2r * a1i + a2i * a1r
    br = a2r * b1r - a2i * b1i + b2r
    bi = a2r * b1i + a2i * b1r + b2i
    return ar, ai, br, bi


def s5_mixer(u, a_re, a_im, log_step, b_re, b_im, c_re, c_im, d, w_glu, b_glu, h_re, h_im):
    B, L, _ = u.shape
    f32 = jnp.float32
    a_re, a_im = a_re.astype(f32), a_im.astype(f32)
    dt = jnp.exp(log_step.astype(f32))[:, None]
    mag = jnp.exp(a_re * dt)
    ang = a_im * dt
    lb_re, lb_im = mag * jnp.cos(ang), mag * jnp.sin(ang)
    den = a_re * a_re + a_im * a_im
    num_re, num_im = lb_re - 1.0, lb_im
    f_re = (num_re * a_re + num_im * a_im) / den
    f_im = (num_im * a_re - num_re * a_im) / den
    b_re, b_im = b_re.astype(f32), b_im.astype(f32)
    bb_re = f_re[..., None] * b_re - f_im[..., None] * b_im
    bb_im = f_re[..., None] * b_im + f_im[..., None] * b_re
    c_re, c_im = c_re.astype(f32), c_im.astype(f32)
    d_g = d.astype(f32).reshape(N_S5_GROUPS, S5_GROUP)
    C = S5_BLOCK if L % S5_BLOCK == 0 else L
    n = L // C
    uc = u.astype(f32).reshape(B, n, C, N_S5_GROUPS, S5_GROUP).transpose(1, 2, 0, 3, 4)

    def step(carry, u_c):
        hr, hi = carry
        bre = jnp.einsum('gnc,lbgc->lbgn', bb_re, u_c)
        bim = jnp.einsum('gnc,lbgc->lbgn', bb_im, u_c)
        bre = bre.at[0].add(lb_re * hr - lb_im * hi)
        bim = bim.at[0].add(lb_re * hi + lb_im * hr)
        are = jnp.broadcast_to(lb_re, bre.shape)
        aim = jnp.broadcast_to(lb_im, bim.shape)
        _, _, xre, xim = lax.associative_scan(s5_combine, (are, aim, bre, bim), axis=0)
        y = (jnp.einsum('gcn,lbgn->lbgc', c_re, xre) - jnp.einsum('gcn,lbgn->lbgc', c_im, xim)
             + d_g * u_c)
        return (xre[-1], xim[-1]), y

    (hr, hi), y = lax.scan(step, (h_re, h_im), uc)
    y = y.transpose(2, 0, 1, 3, 4).reshape(B, L, S5_WIDTH)
    y = jax.nn.gelu(y)
    y = y * jax.nn.sigmoid(y @ w_glu + b_glu)
    return y.astype(u.dtype), hr, hi


def block(x, S0, conv0, h0_re, h0_im, n_mix, w_in, conv_w, a_log, dt_bias, d_norm,
          a_re, a_im, log_step, b_re, b_im, c_re, c_im, s5_d, w_glu, b_glu, w_out,
          n_ffn, w_ffn_in, w_ffn_out):
    f32 = jnp.float32
    B, L, _ = x.shape
    h = rmsnorm(x, n_mix)
    proj = h @ w_in
    o1 = 3 * DELTA_WIDTH
    o2 = o1 + DELTA_WIDTH
    o3 = o2 + N_DELTA_HEADS
    o4 = o3 + N_DELTA_HEADS
    qkv, z, b_raw, a_raw, u = (proj[..., :o1], proj[..., o1:o2], proj[..., o2:o3],
                               proj[..., o3:o4], proj[..., o4:])
    qkv, conv_new = causal_conv(qkv, conv0, conv_w)
    qkv = jax.nn.silu(qkv.astype(f32))
    q, k, v = jnp.split(qkv, 3, axis=-1)
    shp = (B, L, N_DELTA_HEADS, DELTA_HEAD_DIM)
    q = l2norm(q.reshape(shp)) * (DELTA_HEAD_DIM ** -0.5)
    k = l2norm(k.reshape(shp))
    v = v.reshape(shp)
    beta = jax.nn.sigmoid(b_raw.astype(f32))
    g = -jnp.exp(a_log.astype(f32)) * jax.nn.softplus(a_raw.astype(f32) + dt_bias.astype(f32))
    o, S_new = gated_delta_rule(q, k, v, g, beta, S0.astype(f32))
    o = rmsnorm(o, d_norm) * jax.nn.silu(z.reshape(shp).astype(f32))
    y_delta = o.reshape(B, L, DELTA_WIDTH).astype(x.dtype)
    y_s5, hr, hi = s5_mixer(u, a_re, a_im, log_step, b_re, b_im, c_re, c_im, s5_d, w_glu, b_glu,
                            h0_re.astype(f32), h0_im.astype(f32))
    x = x + jnp.concatenate([y_delta, y_s5], axis=-1) @ w_out
    h = rmsnorm(x, n_ffn)
    gate, up = jnp.split(h @ w_ffn_in, 2, axis=-1)
    x = x + (jax.nn.silu(gate) * up) @ w_ffn_out
    return x, S_new.astype(x.dtype), conv_new, hr.astype(x.dtype), hi.astype(x.dtype)


def setup_inputs(seed: int = 0) -> dict:
    key = jax.random.key(seed)
    ks = jax.random.split(key, 32)
    f32 = jnp.float32

    def nrm(i, shape, scale):
        return jax.random.normal(ks[i], shape, f32) * scale

    def unif(i, shape, lo, hi):
        return jax.random.uniform(ks[i], shape, f32, lo, hi)

    H, G, N = N_DELTA_HEADS, N_S5_GROUPS, S5_STATE
    dt = jnp.exp(unif(7, (DEPTH, H), math.log(1e-3), math.log(1e-1)))
    n_idx = jnp.arange(N, dtype=f32)
    return {
        'x_prompt': nrm(0, (BATCH, SEQ, D_MODEL), 1.0),
        'x_sample': nrm(1, (DEC_BATCH, DEC_SEQ, D_MODEL), 1.0),
        'state_delta': nrm(2, (DEPTH, DEC_BATCH, H, DELTA_HEAD_DIM, DELTA_HEAD_DIM), 0.1),
        'state_conv': nrm(3, (DEPTH, DEC_BATCH, CONV_WIDTH - 1, 3 * DELTA_WIDTH), 1.0),
        'state_s5_re': nrm(4, (DEPTH, DEC_BATCH, G, N), 0.5),
        'state_s5_im': nrm(5, (DEPTH, DEC_BATCH, G, N), 0.5),
        'norm_mix': 1.0 + nrm(6, (DEPTH, D_MODEL), 0.02),
        'w_in': nrm(8, (DEPTH, D_MODEL, IN_WIDTH), D_MODEL ** -0.5),
        'conv_w': nrm(9, (DEPTH, CONV_WIDTH, 3 * DELTA_WIDTH), CONV_WIDTH ** -0.5),
        'delta_a_log': jnp.log(unif(10, (DEPTH, H), 1.0, 16.0)),
        'delta_dt_bias': dt + jnp.log(-jnp.expm1(-dt)),
        'delta_norm': 1.0 + nrm(11, (DEPTH, DELTA_HEAD_DIM), 0.02),
        's5_a_re': -0.5 + nrm(12, (DEPTH, G, N), 0.01),
        's5_a_im': math.pi * n_idx + nrm(13, (DEPTH, G, N), 0.01),
        's5_log_step': unif(14, (DEPTH, G), math.log(1e-3), math.log(1e-1)),
        's5_b_re': nrm(15, (DEPTH, G, N, S5_GROUP), (2 * S5_GROUP) ** -0.5),
        's5_b_im': nrm(16, (DEPTH, G, N, S5_GROUP), (2 * S5_GROUP) ** -0.5),
        's5_c_re': nrm(17, (DEPTH, G, S5_GROUP, N), (2 * N) ** -0.5),
        's5_c_im': nrm(18, (DEPTH, G, S5_GROUP, N), (2 * N) ** -0.5),
        's5_d': nrm(19, (DEPTH, S5_WIDTH), 1.0),
        's5_w_glu': nrm(20, (DEPTH, S5_WIDTH, S5_WIDTH), S5_WIDTH ** -0.5),
        's5_b_glu': nrm(21, (DEPTH, S5_WIDTH), 0.02),
        'w_out': nrm(22, (DEPTH, D_MODEL, D_MODEL), D_MODEL ** -0.5),
        'norm_ffn': 1.0 + nrm(23, (DEPTH, D_MODEL), 0.02),
        'w_ffn_in': nrm(24, (DEPTH, D_MODEL, 2 * D_FF), D_MODEL ** -0.5),
        'w_ffn_out': nrm(25, (DEPTH, D_FF, D_MODEL), D_FF ** -0.5),
        'norm_final': 1.0 + nrm(26, (D_MODEL,), 0.02),
    }


def reference(x_prompt, x_sample, state_delta, state_conv, state_s5_re, state_s5_im,
              norm_mix, w_in, conv_w, delta_a_log, delta_dt_bias, delta_norm,
              s5_a_re, s5_a_im, s5_log_step, s5_b_re, s5_b_im, s5_c_re, s5_c_im,
              s5_d, s5_w_glu, s5_b_glu, w_out, norm_ffn, w_ffn_in, w_ffn_out, norm_final):
    f32 = jnp.float32
    bp = x_prompt.shape[0]
    zero_delta = jnp.zeros((bp, N_DELTA_HEADS, DELTA_HEAD_DIM, DELTA_HEAD_DIM), f32)
    zero_conv = jnp.zeros((bp, CONV_WIDTH - 1, 3 * DELTA_WIDTH), x_prompt.dtype)
    zero_s5 = jnp.zeros((bp, N_S5_GROUPS, S5_STATE), f32)
    xp, xs = x_prompt, x_sample
    dp, cp, rp, ip = [], [], [], []
    ds, cs, rs, is_ = [], [], [], []
    for l in range(DEPTH):
        p = (norm_mix[l], w_in[l], conv_w[l], delta_a_log[l], delta_dt_bias[l], delta_norm[l],
             s5_a_re[l], s5_a_im[l], s5_log_step[l], s5_b_re[l], s5_b_im[l], s5_c_re[l], s5_c_im[l],
             s5_d[l], s5_w_glu[l], s5_b_glu[l], w_out[l], norm_ffn[l], w_ffn_in[l], w_ffn_out[l])
        xp, s_d, s_c, s_r, s_i = block(xp, zero_delta, zero_conv, zero_s5, zero_s5, *p)
        dp.append(s_d); cp.append(s_c); rp.append(s_r); ip.append(s_i)
        xs, s_d, s_c, s_r, s_i = block(xs, state_delta[l], state_conv[l], state_s5_re[l],
                                       state_s5_im[l], *p)
        ds.append(s_d); cs.append(s_c); rs.append(s_r); is_.append(s_i)
    y_prompt = rmsnorm(xp, norm_final)
    y_sample = rmsnorm(xs, norm_final)
    return (y_prompt, y_sample,
            jnp.stack(dp), jnp.stack(cp), jnp.stack(rp), jnp.stack(ip),
            jnp.stack(ds), jnp.stack(cs), jnp.stack(rs), jnp.stack(is_))
```

```python
import functools

import jax
import jax.numpy as jnp
from jax import lax
from jax.experimental import pallas as pl
from jax.experimental.pallas import tpu as pltpu

F32 = jnp.float32
BF16 = jnp.bfloat16
EPS = 1e-6

LANES = 128
SUBLANES = 8
VMEM_LIMIT_BYTES = 56 * 1024 * 1024
HEAD_DIM = 128
DELTA_CHUNK = 64
SOLVE_BLOCK = 16
S5_GROUPS_PER_BLOCK = 8
GATE_TILES = 2
HIGHEST = lax.Precision.HIGHEST


def _cparams(n_axes):
    return pltpu.CompilerParams(dimension_semantics=("arbitrary",) * n_axes,
                                vmem_limit_bytes=VMEM_LIMIT_BYTES)


def _pick_tile(n, target, mult):
    best = None
    for d in range(mult, min(n, target) + 1, mult):
        if n % d == 0:
            best = d
    return n if best is None else best


def _idiv_pow2(x, c):
    shift = c.bit_length() - 1
    assert c == 1 << shift
    return lax.shift_right_logical(x, jnp.int32(shift))


def _sigmoid(x):
    return 1.0 / (1.0 + jnp.exp(-x))


def _silu(x):
    return x * _sigmoid(x)


def _softplus(x):
    return jnp.maximum(x, 0.0) + jnp.log(1.0 + jnp.exp(-jnp.abs(x)))


def _gelu_tanh(x):
    c = 0.7978845608028654
    return 0.5 * x * (1.0 + jnp.tanh(c * (x + 0.044715 * (x * x * x))))


def _dot(a, b, precision=None):
    return jnp.dot(a, b, preferred_element_type=F32, precision=precision)


def _dot_nt(a, b, precision=None):
    return lax.dot_general(a, b, (((1,), (1,)), ((), ())), preferred_element_type=F32,
                           precision=precision)


def _dot_tn(a, b, precision=None):
    return lax.dot_general(a, b, (((0,), (0,)), ((), ())), preferred_element_type=F32,
                           precision=precision)


def _rmsnorm_kernel(x_ref, w_ref, o_ref):
    x = x_ref[...]
    y = x * lax.rsqrt(jnp.mean(x * x, axis=-1, keepdims=True) + EPS)
    o_ref[...] = (y * w_ref[...]).astype(o_ref.dtype)


def rmsnorm_rows(x, w, out_dtype):
    m, d = x.shape
    tr = _pick_tile(m, 512, 16)
    return pl.pallas_call(
        _rmsnorm_kernel,
        grid=(m // tr,),
        in_specs=[pl.BlockSpec((tr, d), lambda i: (i, 0)),
                  pl.BlockSpec((1, d), lambda i: (0, 0))],
        out_specs=pl.BlockSpec((tr, d), lambda i: (i, 0)),
        out_shape=jax.ShapeDtypeStruct((m, d), out_dtype),
        compiler_params=_cparams(1),
        name="rmsnorm_rows",
    )(x, w.reshape(1, d))


def _mm_kernel(*refs, n_w, has_res, k_total, tk):
    x_ref = refs[0]
    w_refs = refs[1:1 + n_w]
    pos = 1 + n_w
    res_ref = refs[pos] if has_res else None
    pos += int(has_res)
    o_ref = refs[pos]
    acc_refs = refs[pos + 1:]
    k = pl.program_id(2)
    nk = pl.num_programs(2)
    single_k = tk == k_total

    def finish(parts):
        y = parts[0]
        if n_w == 2:
            y = _silu(y) * parts[1]
        if has_res:
            y = y + res_ref[...]
        o_ref[...] = y.astype(o_ref.dtype)

    x = x_ref[...]
    ragged = (k_total % tk) != 0
    if ragged:
        valid = k_total - k * tk
        x = jnp.where(lax.broadcasted_iota(jnp.int32, x.shape, 1) < valid, x, jnp.zeros_like(x))
    parts = []
    for w_ref in w_refs:
        w = w_ref[...]
        if ragged:
            w = jnp.where(lax.broadcasted_iota(jnp.int32, w.shape, 0) < valid, w, 0.0)
        parts.append(_dot(x, w.astype(BF16)))
    if single_k:
        finish(parts)
        return

    @pl.when(k == 0)
    def _init():
        for a, p in zip(acc_refs, parts):
            a[...] = p

    @pl.when(k > 0)
    def _accumulate():
        for a, p in zip(acc_refs, parts):
            a[...] += p

    @pl.when(k == nk - 1)
    def _finish():
        finish([a[...] for a in acc_refs])


def matmul(x, w, *, n, col_off=0, col_off2=None, res=None, out_dtype=F32,
           tm=1040, tn=512, tk=None):
    m, k_total = x.shape
    assert w.shape[0] == k_total
    tm = _pick_tile(m, tm, 16)
    tn = _pick_tile(n, tn, LANES)
    tk = k_total if tk is None else tk
    assert col_off % tn == 0 and (col_off2 is None or col_off2 % tn == 0)
    n_w = 1 if col_off2 is None else 2
    nk = pl.cdiv(k_total, tk)
    grid = (n // tn, m // tm, nk)
    in_specs = [pl.BlockSpec((tm, tk), lambda j, i, k: (i, k))]
    args = [x]
    for off in (col_off, col_off2)[:n_w]:
        in_specs.append(pl.BlockSpec((tk, tn), lambda j, i, k, o=off // tn: (k, j + o)))
        args.append(w)
    if res is not None:
        in_specs.append(pl.BlockSpec((tm, tn), lambda j, i, k: (i, j)))
        args.append(res)
    kern = functools.partial(_mm_kernel, n_w=n_w, has_res=res is not None,
                             k_total=k_total, tk=tk)
    return pl.pallas_call(
        kern,
        grid=grid,
        in_specs=in_specs,
        out_specs=pl.BlockSpec((tm, tn), lambda j, i, k: (i, j)),
        out_shape=jax.ShapeDtypeStruct((m, n), out_dtype),
        scratch_shapes=[pltpu.VMEM((tm, tn), F32) for _ in range(n_w if nk > 1 else 0)],
        compiler_params=_cparams(3),
        name="matmul_rows",
    )(*args)


def _lane_pick(x, idx):
    lane = lax.broadcasted_iota(jnp.int32, x.shape, 1)
    return jnp.sum(jnp.where(lane == idx, x, 0.0), axis=-1, keepdims=True)


def _unit_lower_inverse(a_strict, c):
    assert c == 4 * SOLVE_BLOCK
    row = lax.broadcasted_iota(jnp.int32, (c, c), 0)
    col = lax.broadcasted_iota(jnp.int32, (c, c), 1)
    eye = jnp.where(row == col, 1.0, 0.0).astype(F32)
    same_block = _idiv_pow2(row, SOLVE_BLOCK) == _idiv_pow2(col, SOLVE_BLOCK)
    d = jnp.where(same_block, a_strict, 0.0)
    e = a_strict - d
    p = -d
    t = eye + p
    for _ in range(3):
        p = _dot(p, p, HIGHEST)
        t = t + _dot(t, p, HIGHEST)
    f = _dot(t, e, HIGHEST)
    f2 = _dot(f, f, HIGHEST)
    g = eye - f + f2 - _dot(f, f2, HIGHEST)
    return _dot(g, t, HIGHEST)


def _delta_prompt_kernel(q_ref, k_ref, v_ref, z_ref, ba_ref, cwq_ref, cwk_ref, cwv_ref,
                         c0q_ref, c0k_ref, c0v_ref, s0_ref, alog_ref, dtb_ref, dn_ref,
                         y_ref, sfin_ref, cnq_ref, cnk_ref, cnv_ref,
                         s_scr, xbuf, o_scr, *, n_heads, lc):
    h = pl.program_id(1)
    c = pl.program_id(2)
    nc = pl.num_programs(2)
    C = DELTA_CHUNK
    PAD = SUBLANES

    @pl.when(c == 0)
    def _init():
        s_scr[...] = s0_ref[...]
        xbuf[0, PAD - 3:PAD, :] = c0q_ref[...]
        xbuf[1, PAD - 3:PAD, :] = c0k_ref[...]
        xbuf[2, PAD - 3:PAD, :] = c0v_ref[...]

    def conv_silu(j, x_ref, cw_ref, cn_ref):
        xbuf[j, PAD:PAD + lc, :] = x_ref[...]
        cw = cw_ref[...]
        acc = xbuf[j, PAD - 3:PAD - 3 + lc, :] * cw[0:1, :]
        acc = acc + xbuf[j, PAD - 2:PAD - 2 + lc, :] * cw[1:2, :]
        acc = acc + xbuf[j, PAD - 1:PAD - 1 + lc, :] * cw[2:3, :]
        acc = acc + xbuf[j, PAD:PAD + lc, :] * cw[3:4, :]
        tail = xbuf[j, PAD + lc - 3:PAD + lc, :]
        xbuf[j, PAD - 3:PAD, :] = tail

        @pl.when(c == nc - 1)
        def _():
            cn_ref[...] = tail

        return _silu(acc)

    q = conv_silu(0, q_ref, cwq_ref, cnq_ref)
    k = conv_silu(1, k_ref, cwk_ref, cnk_ref)
    v = conv_silu(2, v_ref, cwv_ref, cnv_ref)
    q = q * lax.rsqrt(jnp.sum(q * q, axis=-1, keepdims=True) + EPS) * (HEAD_DIM ** -0.5)
    k = k * lax.rsqrt(jnp.sum(k * k, axis=-1, keepdims=True) + EPS)

    ba = ba_ref[...]
    beta = _sigmoid(_lane_pick(ba, h))
    a_raw = _lane_pick(ba, h + n_heads)
    a_log = _lane_pick(alog_ref[...], h)
    dt_b = _lane_pick(dtb_ref[...], h)
    g = -jnp.exp(a_log) * _softplus(a_raw + dt_b)

    row = lax.broadcasted_iota(jnp.int32, (lc, lc), 0)
    col = lax.broadcasted_iota(jnp.int32, (lc, lc), 1)
    tri = jnp.where((_idiv_pow2(row, C) == _idiv_pow2(col, C)) & (row >= col), 1.0, 0.0).astype(F32)
    gc_all = _dot(tri, jnp.broadcast_to(g, (lc, LANES)), HIGHEST)

    crow = lax.broadcasted_iota(jnp.int32, (C, C), 0)
    ccol = lax.broadcasted_iota(jnp.int32, (C, C), 1)
    eye_c = jnp.where(crow == ccol, 1.0, 0.0).astype(F32)
    ones_c = jnp.ones((C, C), F32)

    kb_all = k * beta
    vb_all = v * beta

    s = s_scr[...]
    for ci in range(lc // C):
        sl = slice(ci * C, (ci + 1) * C)
        gc = gc_all[sl, :]
        gc_sq = gc[:, :C]
        gc_row = _dot(ones_c, eye_c * gc_sq, HIGHEST)
        diff = gc_sq - gc_row
        decay_incl = jnp.exp(jnp.where(crow >= ccol, diff, -jnp.inf))
        decay_strict = jnp.where(crow > ccol, decay_incl, 0.0)
        qc, kc, kb, vb = q[sl], k[sl], kb_all[sl], vb_all[sl]
        kc16 = kc.astype(BF16)
        a_strict = _dot_nt(kb.astype(BF16), kc16) * decay_strict
        attn = _dot_nt(qc.astype(BF16), kc16) * decay_incl
        egc = jnp.exp(gc)
        minv = _unit_lower_inverse(a_strict, C)
        rhs = jnp.concatenate([vb, kb * egc], axis=-1)
        sol = _dot(minv, rhs, HIGHEST)
        value, k_cum = sol[:, :HEAD_DIM], sol[:, HEAD_DIM:]
        gc_last = gc[C - 1:C, :]
        q_dec = qc * egc
        k_dec = kc * jnp.exp(gc_last - gc)
        s16 = s.astype(BF16)
        v_new = value - _dot(k_cum.astype(BF16), s16)
        v_new16 = v_new.astype(BF16)
        o = _dot(q_dec.astype(BF16), s16) + _dot(attn.astype(BF16), v_new16)
        s = s * jnp.exp(gc_last) + _dot_tn(k_dec.astype(BF16), v_new16)
        o_scr[sl, :] = o
    s_scr[...] = s

    @pl.when(c == nc - 1)
    def _():
        sfin_ref[...] = s

    o = o_scr[...]
    o = o * lax.rsqrt(jnp.mean(o * o, axis=-1, keepdims=True) + EPS) * dn_ref[...]
    y_ref[...] = (o * _silu(z_ref[...])).astype(y_ref.dtype)


def delta_prompt(qkvz2d, uba2d, conv_w, conv0, s0, alog_pad, dtb_pad, d_norm, *,
                 seq, batch, n_heads, uba_width):
    H = n_heads
    lc = _pick_tile(seq, 256, DELTA_CHUNK)
    nblk = seq // lc
    qkvz_blocks = 4 * H
    uba_blocks = uba_width // LANES
    gate_block = uba_blocks - GATE_TILES

    def col_spec(part):
        return pl.BlockSpec((lc, LANES), lambda b, h, c: (c, b * qkvz_blocks + part * H + h))

    def cw_spec(part):
        return pl.BlockSpec((4, LANES), lambda b, h, c: (0, part * H + h))

    def c0_spec(part):
        return pl.BlockSpec((None, 3, LANES), lambda b, h, c: (b, 0, part * H + h))

    pad_spec = pl.BlockSpec((1, LANES), lambda b, h, c: (0, 0))
    in_specs = [col_spec(0), col_spec(1), col_spec(2), col_spec(3),
                pl.BlockSpec((lc, LANES), lambda b, h, c: (c, b * uba_blocks + gate_block)),
                cw_spec(0), cw_spec(1), cw_spec(2),
                c0_spec(0), c0_spec(1), c0_spec(2),
                pl.BlockSpec((None, None, HEAD_DIM, HEAD_DIM), lambda b, h, c: (b, h, 0, 0)),
                pad_spec, pad_spec, pad_spec]
    cn_spec = pl.BlockSpec((None, 3, LANES), lambda b, h, c: (b, 0, h))
    out_specs = [pl.BlockSpec((lc, LANES), lambda b, h, c: (c, b * H + h)),
                 pl.BlockSpec((None, None, HEAD_DIM, HEAD_DIM), lambda b, h, c: (b, h, 0, 0)),
                 cn_spec, cn_spec, cn_spec]
    cn_shape = jax.ShapeDtypeStruct((batch, 3, H * HEAD_DIM), F32)
    out_shape = [jax.ShapeDtypeStruct((seq, batch * H * HEAD_DIM), BF16),
                 jax.ShapeDtypeStruct((batch, H, HEAD_DIM, HEAD_DIM), F32),
                 cn_shape, cn_shape, cn_shape]
    kern = functools.partial(_delta_prompt_kernel, n_heads=H, lc=lc)
    return pl.pallas_call(
        kern,
        grid=(batch, H, nblk),
        in_specs=in_specs,
        out_specs=out_specs,
        out_shape=out_shape,
        scratch_shapes=[pltpu.VMEM((HEAD_DIM, HEAD_DIM), F32),
                        pltpu.VMEM((3, SUBLANES + lc, LANES), F32),
                        pltpu.VMEM((lc, LANES), F32)],
        compiler_params=_cparams(3),
        name="delta_prompt",
    )(qkvz2d, qkvz2d, qkvz2d, qkvz2d, uba2d, conv_w, conv_w, conv_w,
      conv0, conv0, conv0, s0, alog_pad, dtb_pad, d_norm)


def _delta_decode_kernel(q_ref, k_ref, v_ref, z_ref, ba_ref, cw_ref, c0_ref, s0_ref,
                         alog_ref, dtb_ref, dn_ref,
                         y_ref, snew_ref, cnew_ref,
                         qn_scr, kn_scr, vn_scr, o_scr, bg_scr, *, n_heads, bb):
    H = n_heads
    W = H * HEAD_DIM
    cw = cw_ref[...]

    def conv_silu(part, x_ref):
        x = x_ref[...]
        sl = slice(part * W, (part + 1) * W)
        acc = c0_ref[0, :, sl] * cw[0:1, sl]
        acc = acc + c0_ref[1, :, sl] * cw[1:2, sl]
        acc = acc + c0_ref[2, :, sl] * cw[2:3, sl]
        acc = acc + x * cw[3:4, sl]
        cnew_ref[0, :, sl] = c0_ref[1, :, sl]
        cnew_ref[1, :, sl] = c0_ref[2, :, sl]
        cnew_ref[2, :, sl] = x
        return _silu(acc)

    q = conv_silu(0, q_ref)
    k = conv_silu(1, k_ref)
    v = conv_silu(2, v_ref)
    for h in range(H):
        hs = slice(h * HEAD_DIM, (h + 1) * HEAD_DIM)
        qh, kh = q[:, hs], k[:, hs]
        qn_scr[h] = qh * lax.rsqrt(jnp.sum(qh * qh, axis=-1, keepdims=True) + EPS) * (HEAD_DIM ** -0.5)
        kn_scr[h] = kh * lax.rsqrt(jnp.sum(kh * kh, axis=-1, keepdims=True) + EPS)
        vn_scr[h] = v[:, hs]

    ba = ba_ref[...]
    lane = lax.broadcasted_iota(jnp.int32, ba.shape, 1)
    beta_all = _sigmoid(ba)
    g_all = -jnp.exp(alog_ref[...]) * _softplus(ba + dtb_ref[...])
    bg_scr[...] = jnp.where(lane < H, beta_all, g_all)

    zero7 = jnp.zeros((SUBLANES - 1, HEAD_DIM), F32)
    zero6 = jnp.zeros((SUBLANES - 2, HEAD_DIM), F32)

    def per_seq(bi, carry):
        bg = bg_scr[pl.ds(bi, 1), :]
        for h in range(H):
            k_row = kn_scr[h, pl.ds(bi, 1), :]
            q_row = qn_scr[h, pl.ds(bi, 1), :]
            v_row = vn_scr[h, pl.ds(bi, 1), :]
            beta = bg[:, h:h + 1]
            eg = jnp.exp(bg[:, H + h:H + h + 1])
            s = s0_ref[bi, h]
            lhs = jnp.concatenate([k_row, q_row, zero6], axis=0)
            ks_qs = _dot(lhs, s, HIGHEST)
            ks, qs = ks_qs[0:1, :], ks_qs[1:2, :]
            v_new = beta * v_row - (beta * eg) * ks
            qk = jnp.sum(q_row * k_row, axis=-1, keepdims=True)
            o = eg * qs + qk * v_new
            outer = _dot_tn(jnp.concatenate([k_row, zero7], axis=0),
                            jnp.concatenate([v_new, zero7], axis=0), HIGHEST)
            snew_ref[bi, h] = s * eg + outer
            o_scr[h, pl.ds(bi, 1), :] = o
        return carry

    lax.fori_loop(0, bb, per_seq, 0)

    dn = dn_ref[...]
    z = z_ref[...]
    for h in range(H):
        hs = slice(h * HEAD_DIM, (h + 1) * HEAD_DIM)
        o = o_scr[h]
        o = o * lax.rsqrt(jnp.mean(o * o, axis=-1, keepdims=True) + EPS) * dn
        y_ref[:, hs] = (o * _silu(z[:, hs])).astype(y_ref.dtype)


def delta_decode(qkvz, uba, conv_w, conv0_t, s0, alog_pad2, dtb_pad2, d_norm, *,
                 row_off, n_seq, n_heads):
    H = n_heads
    W = H * HEAD_DIM
    bb = SUBLANES
    assert n_seq % bb == 0 and row_off % bb == 0
    ro = row_off // bb
    gate_block = uba.shape[1] // LANES - GATE_TILES

    def col_spec(part):
        return pl.BlockSpec((bb, W), lambda i: (ro + i, part))

    pad_spec = pl.BlockSpec((1, LANES), lambda i: (0, 0))
    in_specs = [col_spec(0), col_spec(1), col_spec(2), col_spec(3),
                pl.BlockSpec((bb, LANES), lambda i: (ro + i, gate_block)),
                pl.BlockSpec((4, 3 * W), lambda i: (0, 0)),
                pl.BlockSpec((3, bb, 3 * W), lambda i: (0, i, 0)),
                pl.BlockSpec((bb, H, HEAD_DIM, HEAD_DIM), lambda i: (i, 0, 0, 0)),
                pad_spec, pad_spec, pad_spec]
    out_specs = [pl.BlockSpec((bb, W), lambda i: (i, 0)),
                 pl.BlockSpec((bb, H, HEAD_DIM, HEAD_DIM), lambda i: (i, 0, 0, 0)),
                 pl.BlockSpec((3, bb, 3 * W), lambda i: (0, i, 0))]
    out_shape = [jax.ShapeDtypeStruct((n_seq, W), BF16),
                 jax.ShapeDtypeStruct((n_seq, H, HEAD_DIM, HEAD_DIM), F32),
                 jax.ShapeDtypeStruct((3, n_seq, 3 * W), F32)]
    kern = functools.partial(_delta_decode_kernel, n_heads=H, bb=bb)
    head_scr = pltpu.VMEM((H, bb, HEAD_DIM), F32)
    return pl.pallas_call(
        kern,
        grid=(n_seq // bb,),
        in_specs=in_specs,
        out_specs=out_specs,
        out_shape=out_shape,
        scratch_shapes=[head_scr, head_scr, head_scr, head_scr, pltpu.VMEM((bb, LANES), F32)],
        compiler_params=_cparams(1),
        name="delta_decode",
    )(qkvz, qkvz, qkvz, qkvz, uba, conv_w, conv0_t, s0, alog_pad2, dtb_pad2, d_norm)


def _s5_prep_kernel(are_ref, aim_ref, ls_ref, bre_ref, bim_ref,
                    lbre_ref, lbim_ref, bbre_ref, bbim_ref, *, n_state, group):
    a_re = are_ref[...]
    a_im = aim_ref[...]
    dt = jnp.exp(ls_ref[...])
    mag = jnp.exp(a_re * dt)
    ang = a_im * dt
    lb_re = mag * jnp.cos(ang)
    lb_im = mag * jnp.sin(ang)
    den = a_re * a_re + a_im * a_im
    num_re = lb_re - 1.0
    num_im = lb_im
    f_re = (num_re * a_re + num_im * a_im) / den
    f_im = (num_im * a_re - num_re * a_im) / den
    lbre_ref[...] = lb_re
    lbim_ref[...] = lb_im
    r = lax.broadcasted_iota(jnp.int32, (n_state, n_state * group), 0)
    cidx = lax.broadcasted_iota(jnp.int32, (n_state, n_state * group), 1)
    expand = jnp.where(_idiv_pow2(cidx, group) == r, 1.0, 0.0).astype(F32)
    fre_x = _dot(f_re, expand, HIGHEST)
    fim_x = _dot(f_im, expand, HIGHEST)
    b_re = bre_ref[...]
    b_im = bim_ref[...]
    bbre_ref[...] = fre_x * b_re - fim_x * b_im
    bbim_ref[...] = fre_x * b_im + fim_x * b_re


def s5_prep(a_re, a_im, log_step, b_re, b_im):
    g, n = a_re.shape
    group = b_re.shape[-1]
    full = lambda shape: pl.BlockSpec(shape, lambda: (0,) * len(shape))
    kern = functools.partial(_s5_prep_kernel, n_state=n, group=group)
    lb_re, lb_im, bb_re, bb_im = pl.pallas_call(
        kern,
        in_specs=[full((g, n)), full((g, n)), full((g, 1)), full((g, n * group)), full((g, n * group))],
        out_specs=[full((g, n)), full((g, n)), full((g, n * group)), full((g, n * group))],
        out_shape=[jax.ShapeDtypeStruct((g, n), F32), jax.ShapeDtypeStruct((g, n), F32),
                   jax.ShapeDtypeStruct((g, n * group), F32), jax.ShapeDtypeStruct((g, n * group), F32)],
        compiler_params=pltpu.CompilerParams(vmem_limit_bytes=VMEM_LIMIT_BYTES),
        name="s5_prep",
    )(a_re, a_im, log_step.reshape(g, 1), b_re.reshape(g, n * group), b_im.reshape(g, n * group))
    return lb_re, lb_im, bb_re.reshape(g, n, group), bb_im.reshape(g, n, group)


def _s5_kernel(u_ref, h0r_ref, h0i_ref, lbr_ref, lbi_ref, wbr_ref, wbi_ref, wc_ref,
               d_ref, wg_ref, bg_ref,
               y_ref, hr_out, hi_out,
               xr_scr, xi_scr, hr_scr, hi_scr, yc_scr, *, tb, n_blocks, blk_in, blk_state):
    t_blk = pl.program_id(1)
    nt = pl.num_programs(1)
    R = SUBLANES

    @pl.when(t_blk == 0)
    def _init():
        hr_scr[...] = h0r_ref[...]
        hi_scr[...] = h0i_ref[...]

    u = u_ref[...]
    u16 = u.astype(BF16)
    for j in range(n_blocks):
        uj = u16[:, j * blk_in:(j + 1) * blk_in]
        xr_scr[:, j * blk_state:(j + 1) * blk_state] = _dot(uj, wbr_ref[j])
        xi_scr[:, j * blk_state:(j + 1) * blk_state] = _dot(uj, wbi_ref[j])

    n_lanes = n_blocks * blk_state
    lane_chunk = _pick_tile(n_lanes, 1024, LANES)
    for lc in range(n_lanes // lane_chunk):
        ls = slice(lc * lane_chunk, (lc + 1) * lane_chunk)
        lbr = jnp.broadcast_to(lbr_ref[:, ls], (R, lane_chunk))
        lbi = jnp.broadcast_to(lbi_ref[:, ls], (R, lane_chunk))

        def step(t, carry):
            hr, hi = carry
            rows = pl.ds(pl.multiple_of(t * R, R), R)
            xr = lbr * hr - lbi * hi + xr_scr[rows, ls]
            xi = lbr * hi + lbi * hr + xi_scr[rows, ls]
            xr_scr[rows, ls] = xr
            xi_scr[rows, ls] = xi
            return xr, xi

        hr, hi = lax.fori_loop(0, tb, step, (hr_scr[:, ls], hi_scr[:, ls]))
        hr_scr[:, ls] = hr
        hi_scr[:, ls] = hi

    @pl.when(t_blk == nt - 1)
    def _fin():
        hr_out[...] = hr_scr[...]
        hi_out[...] = hi_scr[...]

    for j in range(n_blocks):
        ss = slice(j * blk_state, (j + 1) * blk_state)
        yj = _dot(xr_scr[:, ss].astype(BF16), wc_ref[j, :blk_state, :])
        yj = yj + _dot(xi_scr[:, ss].astype(BF16), wc_ref[j, blk_state:, :])
        yc_scr[:, j * blk_in:(j + 1) * blk_in] = yj
    y = _gelu_tanh(yc_scr[...] + d_ref[...] * u)
    gate = _dot(y.astype(BF16), wg_ref[...]) + bg_ref[...]
    y_ref[...] = (y * _sigmoid(gate)).astype(y_ref.dtype)


def s5_mixer(u_rows, h0_re, h0_im, lb_re, lb_im, wb_re, wb_im, wc, d, w_glu16, b_glu, *,
             n_seq_groups, n_time_blocks, tb):
    rows, width = u_rows.shape
    n_blocks, blk_in, blk_state = wb_re.shape
    n_lanes = n_blocks * blk_state
    rb = tb * SUBLANES
    assert rows == n_seq_groups * n_time_blocks * rb

    def const(shape):
        return pl.BlockSpec(shape, lambda s, t: (0,) * len(shape))

    state_spec = pl.BlockSpec((SUBLANES, n_lanes), lambda s, t: (s, 0))
    in_specs = [pl.BlockSpec((rb, width), lambda s, t: (s * n_time_blocks + t, 0)),
                state_spec, state_spec,
                const((1, n_lanes)), const((1, n_lanes)),
                const(wb_re.shape), const(wb_im.shape), const(wc.shape),
                const((1, width)), const(w_glu16.shape), const((1, width))]
    out_specs = [pl.BlockSpec((rb, width), lambda s, t: (s * n_time_blocks + t, 0)),
                 state_spec, state_spec]
    out_shape = [jax.ShapeDtypeStruct((rows, width), BF16),
                 jax.ShapeDtypeStruct(h0_re.shape, F32),
                 jax.ShapeDtypeStruct(h0_im.shape, F32)]
    kern = functools.partial(_s5_kernel, tb=tb, n_blocks=n_blocks, blk_in=blk_in,
                             blk_state=blk_state)
    return pl.pallas_call(
        kern,
        grid=(n_seq_groups, n_time_blocks),
        in_specs=in_specs,
        out_specs=out_specs,
        out_shape=out_shape,
        scratch_shapes=[pltpu.VMEM((rb, n_lanes), F32), pltpu.VMEM((rb, n_lanes), F32),
                        pltpu.VMEM((SUBLANES, n_lanes), F32), pltpu.VMEM((SUBLANES, n_lanes), F32),
                        pltpu.VMEM((rb, width), F32)],
        compiler_params=_cparams(2),
        name="s5_mixer",
    )(u_rows, h0_re, h0_im, lb_re.reshape(1, n_lanes), lb_im.reshape(1, n_lanes),
      wb_re, wb_im, wc, d.reshape(1, width), w_glu16, b_glu.reshape(1, width))


def _s5_block_weights(bb_re, bb_im, c_re, c_im):
    g, n, grp = bb_re.shape
    r = S5_GROUPS_PER_BLOCK
    nb = g // r
    eye = jnp.eye(r, dtype=F32)

    def in_blocks(bb):
        w = jnp.einsum("jsnc,rs->jrcsn", bb.reshape(nb, r, n, grp), eye)
        return w.reshape(nb, r * grp, r * n).astype(BF16)

    def out_blocks(cc):
        w = jnp.einsum("jrcn,rs->jsnrc", cc.reshape(nb, r, grp, n), eye)
        return w.reshape(nb, r * n, r * grp)

    wc = jnp.concatenate([out_blocks(c_re), -out_blocks(c_im)], axis=1).astype(BF16)
    return in_blocks(bb_re), in_blocks(bb_im), wc


def kernel(x_prompt, x_sample, state_delta, state_conv, state_s5_re, state_s5_im, norm_mix, w_in, conv_w, delta_a_log, delta_dt_bias, delta_norm, s5_a_re, s5_a_im, s5_log_step, s5_b_re, s5_b_im, s5_c_re, s5_c_im, s5_d, s5_w_glu, s5_b_glu, w_out, norm_ffn, w_ffn_in, w_ffn_out, norm_final):
    B, L, D = x_prompt.shape
    Bs = x_sample.shape[0]
    depth = w_in.shape[0]
    H = delta_a_log.shape[1]
    Wd = H * HEAD_DIM
    G, N = s5_a_re.shape[1:]
    grp = s5_b_re.shape[-1]
    Ws = G * grp
    d_ff = w_ffn_out.shape[1]
    Mp = B * L
    M = Mp + Bs
    assert Bs % B == 0 and B <= SUBLANES and Bs % SUBLANES == 0 and x_sample.shape[1] == 1
    uba_w = Ws + GATE_TILES * LANES

    x = jnp.concatenate([jnp.transpose(x_prompt, (1, 0, 2)).reshape(Mp, D),
                         x_sample.reshape(Bs, D)], axis=0)

    gate_cols = 4 * Wd
    w_uba = jnp.concatenate([w_in[:, :, gate_cols + 2 * H:], w_in[:, :, gate_cols:gate_cols + 2 * H],
                             jnp.zeros((depth, D, GATE_TILES * LANES - 2 * H), F32)], axis=2)
    pad_h = lambda a: jnp.pad(a, (0, LANES - H)).reshape(1, LANES)
    pad_h2 = lambda a: jnp.pad(a, (H, LANES - 2 * H)).reshape(1, LANES)
    zero_s = jnp.zeros((B, H, HEAD_DIM, HEAD_DIM), F32)
    zero_c = jnp.zeros((B, 3, 3 * Wd), F32)
    zero_h = jnp.zeros((SUBLANES, G * N), F32)
    s5_tb = _pick_tile(L, 16, 1)

    outs = {k: [] for k in ("dp", "cp", "rp", "ip", "ds", "cs", "rs", "is")}
    for l in range(depth):
        hmix = rmsnorm_rows(x, norm_mix[l], BF16)
        qkvz = matmul(hmix, w_in[l], n=gate_cols)
        uba = matmul(hmix, w_uba[l], n=uba_w)

        qkvz2d = qkvz.reshape(M // B, B * gate_cols)
        uba2d = uba.reshape(M // B, B * uba_w)
        y_dp, s_p, cnq, cnk, cnv = delta_prompt(
            qkvz2d, uba2d, conv_w[l], zero_c, zero_s, pad_h(delta_a_log[l]), pad_h(delta_dt_bias[l]),
            delta_norm[l].reshape(1, HEAD_DIM), seq=L, batch=B, n_heads=H, uba_width=uba_w)
        y_ds, s_s, cn_s = delta_decode(
            qkvz, uba, conv_w[l], jnp.transpose(state_conv[l], (1, 0, 2)), state_delta[l],
            pad_h2(delta_a_log[l]), pad_h2(delta_dt_bias[l]), delta_norm[l].reshape(1, HEAD_DIM),
            row_off=Mp, n_seq=Bs, n_heads=H)
        outs["dp"].append(s_p)
        outs["cp"].append(jnp.concatenate([cnq, cnk, cnv], axis=-1))
        outs["ds"].append(s_s)
        outs["cs"].append(jnp.transpose(cn_s, (1, 0, 2)))

        lb_re, lb_im, bb_re, bb_im = s5_prep(s5_a_re[l], s5_a_im[l], s5_log_step[l],
                                             s5_b_re[l], s5_b_im[l])
        wb_re, wb_im, wc = _s5_block_weights(bb_re, bb_im, s5_c_re[l], s5_c_im[l])
        w_glu16 = s5_w_glu[l].astype(BF16)
        u_p = jnp.pad(uba[:Mp, :Ws].reshape(L, B, Ws), ((0, 0), (0, SUBLANES - B), (0, 0)))
        y_sp, hr_p, hi_p = s5_mixer(
            u_p.reshape(L * SUBLANES, Ws), zero_h, zero_h, lb_re, lb_im, wb_re, wb_im, wc,
            s5_d[l], w_glu16, s5_b_glu[l], n_seq_groups=1, n_time_blocks=L // s5_tb, tb=s5_tb)
        y_ss, hr_s, hi_s = s5_mixer(
            uba[Mp:, :Ws], state_s5_re[l].reshape(Bs, G * N), state_s5_im[l].reshape(Bs, G * N),
            lb_re, lb_im, wb_re, wb_im, wc, s5_d[l], w_glu16, s5_b_glu[l],
            n_seq_groups=Bs // SUBLANES, n_time_blocks=1, tb=1)
        outs["rp"].append(hr_p[:B].reshape(B, G, N))
        outs["ip"].append(hi_p[:B].reshape(B, G, N))
        outs["rs"].append(hr_s.reshape(Bs, G, N))
        outs["is"].append(hi_s.reshape(Bs, G, N))

        y_sp = y_sp.reshape(L, SUBLANES, Ws)[:, :B].reshape(Mp, Ws)
        y_mix = jnp.concatenate([jnp.concatenate([y_dp.reshape(Mp, Wd), y_sp], axis=1),
                                 jnp.concatenate([y_ds, y_ss], axis=1)], axis=0)
        x = matmul(y_mix, w_out[l], n=D, res=x)

        hffn = rmsnorm_rows(x, norm_ffn[l], BF16)
        hidden = matmul(hffn, w_ffn_in[l], n=d_ff, col_off2=d_ff, out_dtype=BF16, tn=256)
        x = matmul(hidden, w_ffn_out[l], n=D, res=x, tn=1024, tk=1024)

    y = rmsnorm_rows(x, norm_final, F32)
    y_prompt = jnp.transpose(y[:Mp].reshape(L, B, D), (1, 0, 2))
    y_sample = y[Mp:].reshape(Bs, 1, D)
    st = lambda k: jnp.stack(outs[k])
    return (y_prompt, y_sample, st("dp"), st("cp"), st("rp"), st("ip"),
            st("ds"), st("cs"), st("rs"), st("is"))
```

```python
import functools

import jax
import jax.numpy as jnp
from jax import lax
from jax.experimental import pallas as pl
from jax.experimental.pallas import tpu as pltpu

F32 = jnp.float32
BF16 = jnp.bfloat16
EPS = 1e-6

LANES = 128
SUBLANES = 8
VMEM_LIMIT_BYTES = 56 * 1024 * 1024
HEAD_DIM = 128
DELTA_CHUNK = 128
SOLVE_BLOCK = 16
S5_GROUPS_PER_BLOCK = 8
GATE_TILES = 2
HIGHEST = lax.Precision.HIGHEST


def _cparams(n_axes):
    return pltpu.CompilerParams(dimension_semantics=("arbitrary",) * n_axes,
                                vmem_limit_bytes=VMEM_LIMIT_BYTES)


def _pick_tile(n, target, mult):
    best = None
    for d in range(mult, min(n, target) + 1, mult):
        if n % d == 0:
            best = d
    return n if best is None else best


def _idiv_pow2(x, c):
    shift = c.bit_length() - 1
    assert c == 1 << shift
    return lax.shift_right_logical(x, jnp.int32(shift))


def _sigmoid(x):
    return 1.0 / (1.0 + jnp.exp(-x))


def _silu(x):
    return x * _sigmoid(x)


def _softplus(x):
    return jnp.maximum(x, 0.0) + jnp.log(1.0 + jnp.exp(-jnp.abs(x)))


def _gelu_tanh(x):
    c = 0.7978845608028654
    return 0.5 * x * (1.0 + jnp.tanh(c * (x + 0.044715 * (x * x * x))))


def _dot(a, b, precision=None):
    return jnp.dot(a, b, preferred_element_type=F32, precision=precision)


def _dot_nt(a, b, precision=None):
    return lax.dot_general(a, b, (((1,), (1,)), ((), ())), preferred_element_type=F32,
                           precision=precision)


def _dot_tn(a, b, precision=None):
    return lax.dot_general(a, b, (((0,), (0,)), ((), ())), preferred_element_type=F32,
                           precision=precision)


def _b16(x):
    return x.astype(BF16)


def _rmsnorm_kernel(x_ref, w_ref, o_ref):
    x = x_ref[...]
    y = x * lax.rsqrt(jnp.mean(x * x, axis=-1, keepdims=True) + EPS)
    o_ref[...] = (y * w_ref[...]).astype(o_ref.dtype)


def rmsnorm_rows(x, w, out_dtype):
    m, d = x.shape
    tr = _pick_tile(m, 512, 16)
    return pl.pallas_call(
        _rmsnorm_kernel,
        grid=(m // tr,),
        in_specs=[pl.BlockSpec((tr, d), lambda i: (i, 0)),
                  pl.BlockSpec((1, d), lambda i: (0, 0))],
        out_specs=pl.BlockSpec((tr, d), lambda i: (i, 0)),
        out_shape=jax.ShapeDtypeStruct((m, d), out_dtype),
        compiler_params=_cparams(1),
        name="rmsnorm_rows",
    )(x, w.reshape(1, d))


def _mm_kernel(*refs, n_w, has_res, k_total, tk):
    x_ref = refs[0]
    w_refs = refs[1:1 + n_w]
    pos = 1 + n_w
    res_ref = refs[pos] if has_res else None
    pos += int(has_res)
    o_ref = refs[pos]
    acc_refs = refs[pos + 1:]
    k = pl.program_id(2)
    nk = pl.num_programs(2)
    single_k = tk == k_total

    def finish(parts):
        y = parts[0]
        if n_w == 2:
            y = _silu(y) * parts[1]
        if has_res:
            y = y + res_ref[...]
        o_ref[...] = y.astype(o_ref.dtype)

    x = x_ref[...]
    ragged = (k_total % tk) != 0
    if ragged:
        valid = k_total - k * tk
        x = jnp.where(lax.broadcasted_iota(jnp.int32, x.shape, 1) < valid, x, jnp.zeros_like(x))
    parts = []
    for w_ref in w_refs:
        w = w_ref[...]
        if ragged:
            w = jnp.where(lax.broadcasted_iota(jnp.int32, w.shape, 0) < valid, w, 0.0)
        parts.append(_dot(x, w.astype(BF16)))
    if single_k:
        finish(parts)
        return

    @pl.when(k == 0)
    def _init():
        for a, p in zip(acc_refs, parts):
            a[...] = p

    @pl.when(k > 0)
    def _accumulate():
        for a, p in zip(acc_refs, parts):
            a[...] += p

    @pl.when(k == nk - 1)
    def _finish():
        finish([a[...] for a in acc_refs])


def matmul(x, w, *, n, col_off=0, col_off2=None, res=None, out_dtype=F32,
           tm=1040, tn=512, tk=None):
    m, k_total = x.shape
    assert w.shape[0] == k_total
    tm = _pick_tile(m, tm, 16)
    tn = _pick_tile(n, tn, LANES)
    tk = k_total if tk is None else tk
    assert col_off % tn == 0 and (col_off2 is None or col_off2 % tn == 0)
    n_w = 1 if col_off2 is None else 2
    nk = pl.cdiv(k_total, tk)
    grid = (n // tn, m // tm, nk)
    in_specs = [pl.BlockSpec((tm, tk), lambda j, i, k: (i, k))]
    args = [x]
    for off in (col_off, col_off2)[:n_w]:
        in_specs.append(pl.BlockSpec((tk, tn), lambda j, i, k, o=off // tn: (k, j + o)))
        args.append(w)
    if res is not None:
        in_specs.append(pl.BlockSpec((tm, tn), lambda j, i, k: (i, j)))
        args.append(res)
    kern = functools.partial(_mm_kernel, n_w=n_w, has_res=res is not None,
                             k_total=k_total, tk=tk)
    return pl.pallas_call(
        kern,
        grid=grid,
        in_specs=in_specs,
        out_specs=pl.BlockSpec((tm, tn), lambda j, i, k: (i, j)),
        out_shape=jax.ShapeDtypeStruct((m, n), out_dtype),
        scratch_shapes=[pltpu.VMEM((tm, tn), F32) for _ in range(n_w if nk > 1 else 0)],
        compiler_params=_cparams(3),
        name="matmul_rows",
    )(*args)


def _lane_pick(x, idx):
    lane = lax.broadcasted_iota(jnp.int32, x.shape, 1)
    return jnp.sum(jnp.where(lane == idx, x, 0.0), axis=-1, keepdims=True)


def _unit_lower_inverse_minus_eye(a_list):
    C = DELTA_CHUNK
    assert C == 8 * SOLVE_BLOCK
    row = lax.broadcasted_iota(jnp.int32, (C, C), 0)
    col = lax.broadcasted_iota(jnp.int32, (C, C), 1)
    same_block = _idiv_pow2(row, SOLVE_BLOCK) == _idiv_pow2(col, SOLVE_BLOCK)
    d = [jnp.where(same_block, a, 0.0) for a in a_list]
    e = [a - dd for a, dd in zip(a_list, d)]
    p = [-dd for dd in d]
    td = list(p)
    for _ in range(3):
        p = [_dot(_b16(x), _b16(x)) for x in p]
        td = [t + x + _dot(_b16(t), _b16(x)) for t, x in zip(td, p)]
    f = [ee + _dot(_b16(t), _b16(ee)) for t, ee in zip(td, e)]
    f2 = [_dot(_b16(x), _b16(x)) for x in f]
    f4 = [_dot(_b16(x), _b16(x)) for x in f2]
    h1 = [x2 - x - _dot(_b16(x), _b16(x2)) for x, x2 in zip(f, f2)]
    gd = [h + x4 + _dot(_b16(h), _b16(x4)) for h, x4 in zip(h1, f4)]
    return [g + t + _dot(_b16(g), _b16(t)) for g, t in zip(gd, td)]


def _delta_prompt_kernel(q_ref, k_ref, v_ref, z_ref, ba_ref, cwq_ref, cwk_ref, cwv_ref,
                         c0q_ref, c0k_ref, c0v_ref, s0_ref, alog_ref, dtb_ref, dn_ref, ymix_any,
                         y_ref, sfin_ref, cnq_ref, cnk_ref, cnv_ref,
                         s_scr, xbuf, ybuf, *, n_heads, lc, batch):
    del ymix_any
    h = pl.program_id(0)
    c = pl.program_id(1)
    nc = pl.num_programs(1)
    C = DELTA_CHUNK
    B = batch
    n_chunks = lc // C
    PAD = SUBLANES

    def rows_of(ref, b):
        return ref[pl.ds(b, lc, stride=B), :]

    @pl.when(c == 0)
    def _init():
        s_scr[...] = s0_ref[...]
        for b in range(B):
            xbuf[0, b, PAD - 3:PAD, :] = c0q_ref[b]
            xbuf[1, b, PAD - 3:PAD, :] = c0k_ref[b]
            xbuf[2, b, PAD - 3:PAD, :] = c0v_ref[b]

    def conv_silu(j, b, x_ref, cw_ref, cn_ref):
        xbuf[j, b, PAD:PAD + lc, :] = rows_of(x_ref, b)
        cw = cw_ref[...]
        acc = xbuf[j, b, PAD - 3:PAD - 3 + lc, :] * cw[0:1, :]
        acc = acc + xbuf[j, b, PAD - 2:PAD - 2 + lc, :] * cw[1:2, :]
        acc = acc + xbuf[j, b, PAD - 1:PAD - 1 + lc, :] * cw[2:3, :]
        acc = acc + xbuf[j, b, PAD:PAD + lc, :] * cw[3:4, :]
        tail = xbuf[j, b, PAD + lc - 3:PAD + lc, :]
        xbuf[j, b, PAD - 3:PAD, :] = tail

        @pl.when(c == nc - 1)
        def _():
            cn_ref[b] = tail

        return _silu(acc)

    a_log = _lane_pick(alog_ref[...], h)
    dt_b = _lane_pick(dtb_ref[...], h)
    neg_a = -jnp.exp(a_log)

    crow = lax.broadcasted_iota(jnp.int32, (C, C), 0)
    ccol = lax.broadcasted_iota(jnp.int32, (C, C), 1)
    lane = lax.broadcasted_iota(jnp.int32, (C, LANES), 1)

    qs, ks, vs, betas, gmat = [], [], [], [], jnp.zeros((C, LANES), F32)
    for b in range(B):
        q = conv_silu(0, b, q_ref, cwq_ref, cnq_ref)
        k = conv_silu(1, b, k_ref, cwk_ref, cnk_ref)
        v = conv_silu(2, b, v_ref, cwv_ref, cnv_ref)
        q = q * lax.rsqrt(jnp.sum(q * q, axis=-1, keepdims=True) + EPS) * (HEAD_DIM ** -0.5)
        k = k * lax.rsqrt(jnp.sum(k * k, axis=-1, keepdims=True) + EPS)
        ba = rows_of(ba_ref, b)
        beta = _sigmoid(_lane_pick(ba, h))
        g = neg_a * _softplus(_lane_pick(ba, h + n_heads) + dt_b)
        for ci in range(n_chunks):
            sl = slice(ci * C, (ci + 1) * C)
            u = b * n_chunks + ci
            qs.append(q[sl]); ks.append(k[sl]); vs.append(v[sl]); betas.append(beta[sl])
            gmat = gmat + jnp.where(lane == u, g[sl], 0.0)
    n_units = B * n_chunks
    assert n_units <= LANES

    tri = jnp.where(crow >= ccol, 1.0, 0.0).astype(F32)
    gc_mat = _dot(tri, gmat, HIGHEST)
    gc_t = jnp.transpose(gc_mat)

    lower_incl = crow >= ccol
    lower_strict = crow > ccol
    gcs = [gc_mat[:, u:u + 1] for u in range(n_units)]
    egc = [jnp.exp(x) for x in gcs]
    kb = [k_ * b_ for k_, b_ in zip(ks, betas)]
    k16 = [_b16(x) for x in ks]
    kq = [_dot_nt(_b16(jnp.concatenate([kb_, q_], axis=0)), k_) for kb_, q_, k_ in zip(kb, qs, k16)]
    decay_incl = [jnp.exp(jnp.where(lower_incl, gcs[u] - gc_t[u:u + 1, :], -jnp.inf))
                  for u in range(n_units)]
    a_strict = [jnp.where(lower_strict, x[:C] * d_, 0.0) for x, d_ in zip(kq, decay_incl)]
    attn16 = [_b16(x[C:] * d_) for x, d_ in zip(kq, decay_incl)]
    md = _unit_lower_inverse_minus_eye(a_strict)
    rhs = [jnp.concatenate([v_ * b_, kb_ * e_], axis=-1)
           for v_, b_, kb_, e_ in zip(vs, betas, kb, egc)]
    sol = [r + _dot(_b16(m), _b16(r)) for m, r in zip(md, rhs)]
    gc_last = [x[C - 1:C, :] for x in gcs]
    kcum_qdec16 = [_b16(jnp.concatenate([s_[:, HEAD_DIM:], q_ * e_], axis=0))
                   for s_, q_, e_ in zip(sol, qs, egc)]
    kdec16 = [_b16(k_ * jnp.exp(gl - g_)) for k_, gl, g_ in zip(ks, gc_last, gcs)]
    g_last = [jnp.exp(x) for x in gc_last]

    s = [s_scr[b] for b in range(B)]
    o_units = [None] * n_units
    for ci in range(n_chunks):
        us = [b * n_chunks + ci for b in range(B)]
        ks_qs = [_dot(kcum_qdec16[u], _b16(s[b])) for b, u in enumerate(us)]
        v_new16 = [_b16(sol[u][:, :HEAD_DIM] - x[:C]) for u, x in zip(us, ks_qs)]
        for b, u in enumerate(us):
            o_units[u] = ks_qs[b][C:] + _dot(attn16[u], v_new16[b])
            s[b] = s[b] * g_last[u] + _dot_tn(kdec16[u], v_new16[b])
    for b in range(B):
        s_scr[b] = s[b]

    @pl.when(c == nc - 1)
    def _():
        for b in range(B):
            sfin_ref[b] = s[b]

    dn = dn_ref[...]
    for b in range(B):
        o = jnp.concatenate(o_units[b * n_chunks:(b + 1) * n_chunks], axis=0)
        o = o * lax.rsqrt(jnp.mean(o * o, axis=-1, keepdims=True) + EPS) * dn
        ybuf[pl.ds(b, lc, stride=B), :] = o * _silu(rows_of(z_ref, b))
    y_ref[...] = ybuf[...].astype(y_ref.dtype)


def delta_prompt(qkvz, uba, conv_w, conv0, s0, alog_pad, dtb_pad, d_norm, ymix, *,
                 seq, batch, n_heads):
    H = n_heads
    B = batch
    lc = _pick_tile(seq, 256, DELTA_CHUNK)
    assert lc % DELTA_CHUNK == 0
    nblk = seq // lc
    gate_block = uba.shape[1] // LANES - GATE_TILES
    rb = lc * B

    def col_spec(part):
        return pl.BlockSpec((rb, LANES), lambda h, c: (c, part * H + h))

    def cw_spec(part):
        return pl.BlockSpec((4, LANES), lambda h, c: (0, part * H + h))

    def c0_spec(part):
        return pl.BlockSpec((B, 3, LANES), lambda h, c: (0, 0, part * H + h))

    pad_spec = pl.BlockSpec((1, LANES), lambda h, c: (0, 0))
    state_spec = pl.BlockSpec((B, None, HEAD_DIM, HEAD_DIM), lambda h, c: (0, h, 0, 0))
    in_specs = [col_spec(0), col_spec(1), col_spec(2), col_spec(3),
                pl.BlockSpec((rb, LANES), lambda h, c: (c, gate_block)),
                cw_spec(0), cw_spec(1), cw_spec(2),
                c0_spec(0), c0_spec(1), c0_spec(2),
                state_spec, pad_spec, pad_spec, pad_spec,
                pl.BlockSpec(memory_space=pl.ANY)]
    cn_spec = pl.BlockSpec((B, 3, LANES), lambda h, c: (0, 0, h))
    out_specs = [pl.BlockSpec((rb, LANES), lambda h, c: (c, h)), state_spec, cn_spec, cn_spec, cn_spec]
    cn_shape = jax.ShapeDtypeStruct((B, 3, H * HEAD_DIM), F32)
    out_shape = [jax.ShapeDtypeStruct(ymix.shape, ymix.dtype),
                 jax.ShapeDtypeStruct((B, H, HEAD_DIM, HEAD_DIM), F32),
                 cn_shape, cn_shape, cn_shape]
    kern = functools.partial(_delta_prompt_kernel, n_heads=H, lc=lc, batch=B)
    return pl.pallas_call(
        kern,
        grid=(H, nblk),
        in_specs=in_specs,
        out_specs=out_specs,
        out_shape=out_shape,
        scratch_shapes=[pltpu.VMEM((B, HEAD_DIM, HEAD_DIM), F32),
                        pltpu.VMEM((3, B, SUBLANES + lc, LANES), F32),
                        pltpu.VMEM((rb, LANES), F32)],
        input_output_aliases={15: 0},
        compiler_params=_cparams(2),
        name="delta_prompt",
    )(qkvz, qkvz, qkvz, qkvz, uba, conv_w, conv_w, conv_w,
      conv0, conv0, conv0, s0, alog_pad, dtb_pad, d_norm, ymix)


def _delta_decode_kernel(q_ref, k_ref, v_ref, z_ref, ba_ref, cw_ref, c0_ref, s0_ref,
                         alog_ref, dtb_ref, dn_ref, ymix_any,
                         y_ref, snew_ref, cnew_ref,
                         qn_scr, kn_scr, vn_scr, o_scr, bg_scr, *, n_heads, bb):
    del ymix_any
    H = n_heads
    W = H * HEAD_DIM
    cw = cw_ref[...]

    def conv_silu(part, x_ref):
        x = x_ref[...]
        sl = slice(part * W, (part + 1) * W)
        acc = c0_ref[0, :, sl] * cw[0:1, sl]
        acc = acc + c0_ref[1, :, sl] * cw[1:2, sl]
        acc = acc + c0_ref[2, :, sl] * cw[2:3, sl]
        acc = acc + x * cw[3:4, sl]
        cnew_ref[0, :, sl] = c0_ref[1, :, sl]
        cnew_ref[1, :, sl] = c0_ref[2, :, sl]
        cnew_ref[2, :, sl] = x
        return _silu(acc)

    q = conv_silu(0, q_ref)
    k = conv_silu(1, k_ref)
    v = conv_silu(2, v_ref)
    for h in range(H):
        hs = slice(h * HEAD_DIM, (h + 1) * HEAD_DIM)
        qh, kh = q[:, hs], k[:, hs]
        qn_scr[h] = qh * lax.rsqrt(jnp.sum(qh * qh, axis=-1, keepdims=True) + EPS) * (HEAD_DIM ** -0.5)
        kn_scr[h] = kh * lax.rsqrt(jnp.sum(kh * kh, axis=-1, keepdims=True) + EPS)
        vn_scr[h] = v[:, hs]

    ba = ba_ref[...]
    lane = lax.broadcasted_iota(jnp.int32, ba.shape, 1)
    beta_all = _sigmoid(ba)
    g_all = -jnp.exp(alog_ref[...]) * _softplus(ba + dtb_ref[...])
    bg_scr[...] = jnp.where(lane < H, beta_all, g_all)

    zero7 = jnp.zeros((SUBLANES - 1, HEAD_DIM), F32)
    zero6 = jnp.zeros((SUBLANES - 2, HEAD_DIM), F32)

    def per_seq(bi, carry):
        bg = bg_scr[pl.ds(bi, 1), :]
        for h in range(H):
            k_row = kn_scr[h, pl.ds(bi, 1), :]
            q_row = qn_scr[h, pl.ds(bi, 1), :]
            v_row = vn_scr[h, pl.ds(bi, 1), :]
            beta = bg[:, h:h + 1]
            eg = jnp.exp(bg[:, H + h:H + h + 1])
            s = s0_ref[bi, h]
            lhs = jnp.concatenate([k_row, q_row, zero6], axis=0)
            ks_qs = _dot(lhs, s, HIGHEST)
            ks, qs = ks_qs[0:1, :], ks_qs[1:2, :]
            v_new = beta * v_row - (beta * eg) * ks
            qk = jnp.sum(q_row * k_row, axis=-1, keepdims=True)
            o = eg * qs + qk * v_new
            outer = _dot_tn(jnp.concatenate([k_row, zero7], axis=0),
                            jnp.concatenate([v_new, zero7], axis=0), HIGHEST)
            snew_ref[bi, h] = s * eg + outer
            o_scr[h, pl.ds(bi, 1), :] = o
        return carry

    lax.fori_loop(0, bb, per_seq, 0)

    dn = dn_ref[...]
    z = z_ref[...]
    for h in range(H):
        hs = slice(h * HEAD_DIM, (h + 1) * HEAD_DIM)
        o = o_scr[h]
        o = o * lax.rsqrt(jnp.mean(o * o, axis=-1, keepdims=True) + EPS) * dn
        y_ref[:, hs] = (o * _silu(z[:, hs])).astype(y_ref.dtype)


def delta_decode(qkvz, uba, conv_w, conv0_t, s0, alog_pad2, dtb_pad2, d_norm, ymix, *,
                 row_off, n_seq, n_heads):
    H = n_heads
    W = H * HEAD_DIM
    bb = SUBLANES
    assert n_seq % bb == 0 and row_off % bb == 0
    ro = row_off // bb
    gate_block = uba.shape[1] // LANES - GATE_TILES

    def col_spec(part):
        return pl.BlockSpec((bb, W), lambda i: (ro + i, part))

    pad_spec = pl.BlockSpec((1, LANES), lambda i: (0, 0))
    in_specs = [col_spec(0), col_spec(1), col_spec(2), col_spec(3),
                pl.BlockSpec((bb, LANES), lambda i: (ro + i, gate_block)),
                pl.BlockSpec((4, 3 * W), lambda i: (0, 0)),
                pl.BlockSpec((3, bb, 3 * W), lambda i: (0, i, 0)),
                pl.BlockSpec((bb, H, HEAD_DIM, HEAD_DIM), lambda i: (i, 0, 0, 0)),
                pad_spec, pad_spec, pad_spec,
                pl.BlockSpec(memory_space=pl.ANY)]
    out_specs = [pl.BlockSpec((bb, W), lambda i: (ro + i, 0)),
                 pl.BlockSpec((bb, H, HEAD_DIM, HEAD_DIM), lambda i: (i, 0, 0, 0)),
                 pl.BlockSpec((3, bb, 3 * W), lambda i: (0, i, 0))]
    out_shape = [jax.ShapeDtypeStruct(ymix.shape, ymix.dtype),
                 jax.ShapeDtypeStruct((n_seq, H, HEAD_DIM, HEAD_DIM), F32),
                 jax.ShapeDtypeStruct((3, n_seq, 3 * W), F32)]
    kern = functools.partial(_delta_decode_kernel, n_heads=H, bb=bb)
    head_scr = pltpu.VMEM((H, bb, HEAD_DIM), F32)
    return pl.pallas_call(
        kern,
        grid=(n_seq // bb,),
        in_specs=in_specs,
        out_specs=out_specs,
        out_shape=out_shape,
        scratch_shapes=[head_scr, head_scr, head_scr, head_scr, pltpu.VMEM((bb, LANES), F32)],
        input_output_aliases={11: 0},
        compiler_params=_cparams(1),
        name="delta_decode",
    )(qkvz, qkvz, qkvz, qkvz, uba, conv_w, conv0_t, s0, alog_pad2, dtb_pad2, d_norm, ymix)


def _s5_prep_kernel(are_ref, aim_ref, ls_ref, bre_ref, bim_ref,
                    lbre_ref, lbim_ref, bbre_ref, bbim_ref, *, n_state, group):
    a_re = are_ref[...]
    a_im = aim_ref[...]
    dt = jnp.exp(ls_ref[...])
    mag = jnp.exp(a_re * dt)
    ang = a_im * dt
    lb_re = mag * jnp.cos(ang)
    lb_im = mag * jnp.sin(ang)
    den = a_re * a_re + a_im * a_im
    num_re = lb_re - 1.0
    num_im = lb_im
    f_re = (num_re * a_re + num_im * a_im) / den
    f_im = (num_im * a_re - num_re * a_im) / den
    lbre_ref[...] = lb_re
    lbim_ref[...] = lb_im
    r = lax.broadcasted_iota(jnp.int32, (n_state, n_state * group), 0)
    cidx = lax.broadcasted_iota(jnp.int32, (n_state, n_state * group), 1)
    expand = jnp.where(_idiv_pow2(cidx, group) == r, 1.0, 0.0).astype(F32)
    fre_x = _dot(f_re, expand, HIGHEST)
    fim_x = _dot(f_im, expand, HIGHEST)
    b_re = bre_ref[...]
    b_im = bim_ref[...]
    bbre_ref[...] = fre_x * b_re - fim_x * b_im
    bbim_ref[...] = fre_x * b_im + fim_x * b_re


def s5_prep(a_re, a_im, log_step, b_re, b_im):
    g, n = a_re.shape
    group = b_re.shape[-1]
    full = lambda shape: pl.BlockSpec(shape, lambda: (0,) * len(shape))
    kern = functools.partial(_s5_prep_kernel, n_state=n, group=group)
    lb_re, lb_im, bb_re, bb_im = pl.pallas_call(
        kern,
        in_specs=[full((g, n)), full((g, n)), full((g, 1)), full((g, n * group)), full((g, n * group))],
        out_specs=[full((g, n)), full((g, n)), full((g, n * group)), full((g, n * group))],
        out_shape=[jax.ShapeDtypeStruct((g, n), F32), jax.ShapeDtypeStruct((g, n), F32),
                   jax.ShapeDtypeStruct((g, n * group), F32), jax.ShapeDtypeStruct((g, n * group), F32)],
        compiler_params=pltpu.CompilerParams(vmem_limit_bytes=VMEM_LIMIT_BYTES),
        name="s5_prep",
    )(a_re, a_im, log_step.reshape(g, 1), b_re.reshape(g, n * group), b_im.reshape(g, n * group))
    return lb_re, lb_im, bb_re.reshape(g, n, group), bb_im.reshape(g, n, group)


def _s5_kernel(u_ref, h0r_ref, h0i_ref, lbr_ref, lbi_ref, wbr_ref, wbi_ref, wc_ref,
               d_ref, wg_ref, bg_ref, ymix_any,
               y_ref, hr_out, hi_out,
               xr_scr, xi_scr, hr_scr, hi_scr, yc_scr, *, rb, single_step, n_blocks, blk_in, blk_state):
    del ymix_any
    t_blk = pl.program_id(1)
    nt = pl.num_programs(1)
    R = SUBLANES
    half = R // 2

    if not single_step:
        @pl.when(t_blk == 0)
        def _init():
            hr_scr[...] = h0r_ref[...]
            hi_scr[...] = h0i_ref[...]

    u = u_ref[...]
    u16 = u.astype(BF16)
    for j in range(n_blocks):
        uj = u16[:, j * blk_in:(j + 1) * blk_in]
        xr_scr[:, j * blk_state:(j + 1) * blk_state] = _dot(uj, wbr_ref[j])
        xi_scr[:, j * blk_state:(j + 1) * blk_state] = _dot(uj, wbi_ref[j])

    n_lanes = n_blocks * blk_state
    lane_chunk = _pick_tile(n_lanes, 1024, LANES)
    low = lax.broadcasted_iota(jnp.int32, (R, lane_chunk), 0) < half
    for lc in range(n_lanes // lane_chunk):
        ls = slice(lc * lane_chunk, (lc + 1) * lane_chunk)
        if single_step:
            lbr, lbi = lbr_ref[:, ls], lbi_ref[:, ls]
            hr, hi = h0r_ref[:, ls], h0i_ref[:, ls]
            xr = lbr * hr - lbi * hi + xr_scr[:, ls]
            xi = lbr * hi + lbi * hr + xi_scr[:, ls]
            xr_scr[:, ls] = xr
            xi_scr[:, ls] = xi
            hr_out[:, ls] = xr
            hi_out[:, ls] = xi
            continue
        lbr = jnp.broadcast_to(lbr_ref[:, ls], (R, lane_chunk))
        lbi = jnp.broadcast_to(lbi_ref[:, ls], (R, lane_chunk))

        def step4(t, carry):
            hr, hi = carry
            rows = pl.ds(pl.multiple_of(t * R, R), R)
            br, bi = xr_scr[rows, ls], xi_scr[rows, ls]
            pr, pi = pltpu.roll(hr, half, 0), pltpu.roll(hi, half, 0)
            y1r = lbr * pr - lbi * pi + br
            y1i = lbr * pi + lbi * pr + bi
            pr, pi = pltpu.roll(y1r, half, 0), pltpu.roll(y1i, half, 0)
            y2r = lbr * pr - lbi * pi + br
            y2i = lbr * pi + lbi * pr + bi
            xr = jnp.where(low, y1r, y2r)
            xi = jnp.where(low, y1i, y2i)
            xr_scr[rows, ls] = xr
            xi_scr[rows, ls] = xi
            return xr, xi

        hr, hi = lax.fori_loop(0, rb // R, step4, (hr_scr[:, ls], hi_scr[:, ls]))
        hr_scr[:, ls] = hr
        hi_scr[:, ls] = hi

    if not single_step:
        @pl.when(t_blk == nt - 1)
        def _fin():
            hr_out[...] = hr_scr[...]
            hi_out[...] = hi_scr[...]

    for j in range(n_blocks):
        ss = slice(j * blk_state, (j + 1) * blk_state)
        yj = _dot(xr_scr[:, ss].astype(BF16), wc_ref[j, :blk_state, :])
        yj = yj + _dot(xi_scr[:, ss].astype(BF16), wc_ref[j, blk_state:, :])
        yc_scr[:, j * blk_in:(j + 1) * blk_in] = yj
    y = _gelu_tanh(yc_scr[...] + d_ref[...] * u)
    gate = _dot(y.astype(BF16), wg_ref[...]) + bg_ref[...]
    y_ref[...] = (y * _sigmoid(gate)).astype(y_ref.dtype)


def s5_mixer(uba, h0_re, h0_im, lb_re, lb_im, wb_re, wb_im, wc, d, w_glu16, b_glu, ymix, *,
             row_off, n_groups, n_time_blocks, rb, single_step, width, col_block):
    n_blocks, blk_in, blk_state = wb_re.shape
    n_lanes = n_blocks * blk_state
    assert row_off % rb == 0 and rb % (2 * SUBLANES) == 0
    assert n_time_blocks == 1 if single_step else n_groups == 1
    ro = row_off // rb

    def const(shape):
        return pl.BlockSpec(shape, lambda s, t: (0,) * len(shape))

    state_spec = pl.BlockSpec((rb if single_step else SUBLANES, n_lanes), lambda s, t: (s, 0))
    in_specs = [pl.BlockSpec((rb, width), lambda s, t: (ro + s * n_time_blocks + t, 0)),
                state_spec, state_spec,
                const((1, n_lanes)), const((1, n_lanes)),
                const(wb_re.shape), const(wb_im.shape), const(wc.shape),
                const((1, width)), const(w_glu16.shape), const((1, width)),
                pl.BlockSpec(memory_space=pl.ANY)]
    out_specs = [pl.BlockSpec((rb, width), lambda s, t: (ro + s * n_time_blocks + t, col_block)),
                 state_spec, state_spec]
    out_shape = [jax.ShapeDtypeStruct(ymix.shape, ymix.dtype),
                 jax.ShapeDtypeStruct(h0_re.shape, F32),
                 jax.ShapeDtypeStruct(h0_im.shape, F32)]
    kern = functools.partial(_s5_kernel, rb=rb, single_step=single_step, n_blocks=n_blocks, blk_in=blk_in,
                             blk_state=blk_state)
    return pl.pallas_call(
        kern,
        grid=(n_groups, n_time_blocks),
        in_specs=in_specs,
        out_specs=out_specs,
        out_shape=out_shape,
        scratch_shapes=[pltpu.VMEM((rb, n_lanes), F32), pltpu.VMEM((rb, n_lanes), F32),
                        pltpu.VMEM((SUBLANES, n_lanes), F32), pltpu.VMEM((SUBLANES, n_lanes), F32),
                        pltpu.VMEM((rb, width), F32)],
        input_output_aliases={11: 0},
        compiler_params=_cparams(2),
        name="s5_mixer",
    )(uba, h0_re, h0_im, lb_re.reshape(1, n_lanes), lb_im.reshape(1, n_lanes),
      wb_re, wb_im, wc, d.reshape(1, width), w_glu16, b_glu.reshape(1, width), ymix)


def _s5_block_weights(bb_re, bb_im, c_re, c_im):
    g, n, grp = bb_re.shape
    r = S5_GROUPS_PER_BLOCK
    nb = g // r
    eye = jnp.eye(r, dtype=F32)

    def in_blocks(bb):
        w = jnp.einsum("jsnc,rs->jrcsn", bb.reshape(nb, r, n, grp), eye)
        return w.reshape(nb, r * grp, r * n).astype(BF16)

    def out_blocks(cc):
        w = jnp.einsum("jrcn,rs->jsnrc", cc.reshape(nb, r, grp, n), eye)
        return w.reshape(nb, r * n, r * grp)

    wc = jnp.concatenate([out_blocks(c_re), -out_blocks(c_im)], axis=1).astype(BF16)
    return in_blocks(bb_re), in_blocks(bb_im), wc


def kernel(x_prompt, x_sample, state_delta, state_conv, state_s5_re, state_s5_im, norm_mix, w_in, conv_w, delta_a_log, delta_dt_bias, delta_norm, s5_a_re, s5_a_im, s5_log_step, s5_b_re, s5_b_im, s5_c_re, s5_c_im, s5_d, s5_w_glu, s5_b_glu, w_out, norm_ffn, w_ffn_in, w_ffn_out, norm_final):
    B, L, D = x_prompt.shape
    Bs = x_sample.shape[0]
    depth = w_in.shape[0]
    H = delta_a_log.shape[1]
    Wd = H * HEAD_DIM
    G, N = s5_a_re.shape[1:]
    grp = s5_b_re.shape[-1]
    Ws = G * grp
    d_ff = w_ffn_out.shape[1]
    Mp = B * L
    M = Mp + Bs
    half = SUBLANES // 2
    assert B == half and x_sample.shape[1] == 1 and Wd % Ws == 0
    uba_w = Ws + GATE_TILES * LANES

    x = jnp.concatenate([jnp.transpose(x_prompt, (1, 0, 2)).reshape(Mp, D),
                         x_sample.reshape(Bs, D)], axis=0)

    gate_cols = 4 * Wd
    w_uba = jnp.concatenate([w_in[:, :, gate_cols + 2 * H:], w_in[:, :, gate_cols:gate_cols + 2 * H],
                             jnp.zeros((depth, D, GATE_TILES * LANES - 2 * H), F32)], axis=2)
    pad_h = lambda a: jnp.pad(a, (0, LANES - H)).reshape(1, LANES)
    pad_h2 = lambda a: jnp.pad(a, (H, LANES - 2 * H)).reshape(1, LANES)
    zero_s = jnp.zeros((B, H, HEAD_DIM, HEAD_DIM), F32)
    zero_c = jnp.zeros((B, 3, 3 * Wd), F32)
    zero_h = jnp.zeros((SUBLANES, G * N), F32)
    s5_rb = _pick_tile(Mp, 128, 2 * SUBLANES)
    s5_rb_dec = _pick_tile(Bs, 32, 2 * SUBLANES)

    outs = {k: [] for k in ("dp", "cp", "rp", "ip", "ds", "cs", "rs", "is")}
    for l in range(depth):
        hmix = rmsnorm_rows(x, norm_mix[l], BF16)
        qkvz = matmul(hmix, w_in[l], n=gate_cols)
        uba = matmul(hmix, w_uba[l], n=uba_w)

        ymix = jnp.zeros((M, Wd + Ws), BF16)
        dnorm = delta_norm[l].reshape(1, HEAD_DIM)
        ymix, s_p, cnq, cnk, cnv = delta_prompt(
            qkvz, uba, conv_w[l], zero_c, zero_s, pad_h(delta_a_log[l]), pad_h(delta_dt_bias[l]),
            dnorm, ymix, seq=L, batch=B, n_heads=H)
        ymix, s_s, cn_s = delta_decode(
            qkvz, uba, conv_w[l], jnp.transpose(state_conv[l], (1, 0, 2)), state_delta[l],
            pad_h2(delta_a_log[l]), pad_h2(delta_dt_bias[l]), dnorm, ymix,
            row_off=Mp, n_seq=Bs, n_heads=H)
        outs["dp"].append(s_p)
        outs["cp"].append(jnp.concatenate([cnq, cnk, cnv], axis=-1))
        outs["ds"].append(s_s)
        outs["cs"].append(jnp.transpose(cn_s, (1, 0, 2)))

        lb_re, lb_im, bb_re, bb_im = s5_prep(s5_a_re[l], s5_a_im[l], s5_log_step[l],
                                             s5_b_re[l], s5_b_im[l])
        wb_re, wb_im, wc = _s5_block_weights(bb_re, bb_im, s5_c_re[l], s5_c_im[l])
        w_glu16 = s5_w_glu[l].astype(BF16)
        ymix, hr_p, hi_p = s5_mixer(
            uba, zero_h, zero_h, lb_re, lb_im, wb_re, wb_im, wc, s5_d[l], w_glu16, s5_b_glu[l], ymix,
            row_off=0, n_groups=1, n_time_blocks=Mp // s5_rb, rb=s5_rb, single_step=False, width=Ws,
            col_block=Wd // Ws)
        ymix, hr_s, hi_s = s5_mixer(
            uba, state_s5_re[l].reshape(Bs, G * N), state_s5_im[l].reshape(Bs, G * N),
            lb_re, lb_im, wb_re, wb_im, wc, s5_d[l], w_glu16, s5_b_glu[l], ymix,
            row_off=Mp, n_groups=Bs // s5_rb_dec, n_time_blocks=1, rb=s5_rb_dec, single_step=True,
            width=Ws, col_block=Wd // Ws)
        outs["rp"].append(hr_p[half:].reshape(B, G, N))
        outs["ip"].append(hi_p[half:].reshape(B, G, N))
        outs["rs"].append(hr_s.reshape(Bs, G, N))
        outs["is"].append(hi_s.reshape(Bs, G, N))

        x = matmul(ymix, w_out[l], n=D, res=x)

        hffn = rmsnorm_rows(x, norm_ffn[l], BF16)
        hidden = matmul(hffn, w_ffn_in[l], n=d_ff, col_off2=d_ff, out_dtype=BF16, tn=256)
        x = matmul(hidden, w_ffn_out[l], n=D, res=x, tn=1024, tk=1024)

    y = rmsnorm_rows(x, norm_final, F32)
    y_prompt = jnp.transpose(y[:Mp].reshape(L, B, D), (1, 0, 2))
    y_sample = y[Mp:].reshape(Bs, 1, D)
    st = lambda k: jnp.stack(outs[k])
    return (y_prompt, y_sample, st("dp"), st("cp"), st("rp"), st("ip"),
            st("ds"), st("cs"), st("rs"), st("is"))
```

```python
import functools

import jax
import jax.numpy as jnp
from jax import lax
from jax.experimental import pallas as pl
from jax.experimental.pallas import tpu as pltpu

F32 = jnp.float32
BF16 = jnp.bfloat16
EPS = 1e-6

LANES = 128
SUBLANES = 8
VMEM_LIMIT_BYTES = 56 * 1024 * 1024
HEAD_DIM = 128
DELTA_CHUNK = 128
SOLVE_BLOCK = 16
S5_GROUPS_PER_BLOCK = 8
GATE_TILES = 4
HIGHEST = lax.Precision.HIGHEST


def _cparams(n_axes):
    return pltpu.CompilerParams(dimension_semantics=("arbitrary",) * n_axes,
                                vmem_limit_bytes=VMEM_LIMIT_BYTES)


def _pick_tile(n, target, mult):
    best = None
    for d in range(mult, min(n, target) + 1, mult):
        if n % d == 0:
            best = d
    return n if best is None else best


def _idiv_pow2(x, c):
    shift = c.bit_length() - 1
    assert c == 1 << shift
    return lax.shift_right_logical(x, jnp.int32(shift))


def _sigmoid(x):
    return 1.0 / (1.0 + jnp.exp(-x))


def _silu(x):
    return x * _sigmoid(x)


def _softplus(x):
    return jnp.maximum(x, 0.0) + jnp.log(1.0 + jnp.exp(-jnp.abs(x)))


def _gelu_tanh(x):
    c = 0.7978845608028654
    return 0.5 * x * (1.0 + jnp.tanh(c * (x + 0.044715 * (x * x * x))))


def _dot(a, b, precision=None):
    return jnp.dot(a, b, preferred_element_type=F32, precision=precision)


def _dot_nt(a, b, precision=None):
    return lax.dot_general(a, b, (((1,), (1,)), ((), ())), preferred_element_type=F32,
                           precision=precision)


def _dot_tn(a, b, precision=None):
    return lax.dot_general(a, b, (((0,), (0,)), ((), ())), preferred_element_type=F32,
                           precision=precision)


def _b16(x):
    return x.astype(BF16)


def _rmsnorm_kernel(x_ref, w_ref, o_ref):
    x = x_ref[...]
    y = x * lax.rsqrt(jnp.mean(x * x, axis=-1, keepdims=True) + EPS)
    o_ref[...] = (y * w_ref[...]).astype(o_ref.dtype)


def rmsnorm_rows(x, w, out_dtype):
    m, d = x.shape
    tr = _pick_tile(m, 512, 16)
    return pl.pallas_call(
        _rmsnorm_kernel,
        grid=(m // tr,),
        in_specs=[pl.BlockSpec((tr, d), lambda i: (i, 0)),
                  pl.BlockSpec((1, d), lambda i: (0, 0))],
        out_specs=pl.BlockSpec((tr, d), lambda i: (i, 0)),
        out_shape=jax.ShapeDtypeStruct((m, d), out_dtype),
        compiler_params=_cparams(1),
        name="rmsnorm_rows",
    )(x, w.reshape(1, d))


def _mm_kernel(*refs, n_w, has_res, k_total, tk):
    x_ref = refs[0]
    w_refs = refs[1:1 + n_w]
    pos = 1 + n_w
    res_ref = refs[pos] if has_res else None
    pos += int(has_res)
    o_ref = refs[pos]
    acc_refs = refs[pos + 1:]
    k = pl.program_id(2)
    nk = pl.num_programs(2)
    single_k = tk == k_total

    def finish(parts):
        y = parts[0]
        if n_w == 2:
            y = _silu(y) * parts[1]
        if has_res:
            y = y + res_ref[...]
        o_ref[...] = y.astype(o_ref.dtype)

    x = x_ref[...]
    ragged = (k_total % tk) != 0
    if ragged:
        valid = k_total - k * tk
        x = jnp.where(lax.broadcasted_iota(jnp.int32, x.shape, 1) < valid, x, jnp.zeros_like(x))
    parts = []
    for w_ref in w_refs:
        w = w_ref[...]
        if ragged:
            w = jnp.where(lax.broadcasted_iota(jnp.int32, w.shape, 0) < valid, w, 0.0)
        parts.append(_dot(x, w.astype(BF16)))
    if single_k:
        finish(parts)
        return

    @pl.when(k == 0)
    def _init():
        for a, p in zip(acc_refs, parts):
            a[...] = p

    @pl.when(k > 0)
    def _accumulate():
        for a, p in zip(acc_refs, parts):
            a[...] += p

    @pl.when(k == nk - 1)
    def _finish():
        finish([a[...] for a in acc_refs])


def matmul(x, w, layer, *, n, col_off=0, col_off2=None, res=None, out_dtype=F32,
           tm=1040, tn=512, tk=None, rows_outer=False):
    m, k_total = x.shape
    assert w.shape[1] == k_total
    tm = _pick_tile(m, tm, 16)
    tn = _pick_tile(n, tn, LANES)
    tk = k_total if tk is None else tk
    assert col_off % tn == 0 and (col_off2 is None or col_off2 % tn == 0)
    n_w = 1 if col_off2 is None else 2
    nk = pl.cdiv(k_total, tk)
    if rows_outer:
        grid = (m // tm, n // tn, nk)
        ij = lambda a, b: (a, b)
    else:
        grid = (n // tn, m // tm, nk)
        ij = lambda a, b: (b, a)
    in_specs = [pl.BlockSpec((tm, tk), lambda a, b, k: (ij(a, b)[0], k))]
    args = [x]
    for off in (col_off, col_off2)[:n_w]:
        in_specs.append(pl.BlockSpec((None, tk, tn),
                                     lambda a, b, k, o=off // tn: (layer, k, ij(a, b)[1] + o)))
        args.append(w)
    if res is not None:
        in_specs.append(pl.BlockSpec((tm, tn), lambda a, b, k: ij(a, b)))
        args.append(res)
    kern = functools.partial(_mm_kernel, n_w=n_w, has_res=res is not None,
                             k_total=k_total, tk=tk)
    return pl.pallas_call(
        kern,
        grid=grid,
        in_specs=in_specs,
        out_specs=pl.BlockSpec((tm, tn), lambda a, b, k: ij(a, b)),
        out_shape=jax.ShapeDtypeStruct((m, n), out_dtype),
        scratch_shapes=[pltpu.VMEM((tm, tn), F32) for _ in range(n_w if nk > 1 else 0)],
        compiler_params=_cparams(3),
        name="matmul_rows",
    )(*args)


def _cast_kernel(x_ref, o_ref):
    o_ref[...] = x_ref[...].astype(o_ref.dtype)


def cast_bf16(w):
    depth, k, n = w.shape
    tr = _pick_tile(k, 512, 16)
    return pl.pallas_call(
        _cast_kernel,
        grid=(depth, k // tr),
        in_specs=[pl.BlockSpec((None, tr, n), lambda l, i: (l, i, 0))],
        out_specs=pl.BlockSpec((None, tr, n), lambda l, i: (l, i, 0)),
        out_shape=jax.ShapeDtypeStruct(w.shape, BF16),
        compiler_params=_cparams(2),
        name="cast_bf16",
    )(w)


def _lane_pick(x, idx):
    lane = lax.broadcasted_iota(jnp.int32, x.shape, 1)
    return jnp.sum(jnp.where(lane == idx, x, 0.0), axis=-1, keepdims=True)


def _unit_lower_inverse_minus_eye(a_list):
    C = DELTA_CHUNK
    assert C == 8 * SOLVE_BLOCK
    row = lax.broadcasted_iota(jnp.int32, (C, C), 0)
    col = lax.broadcasted_iota(jnp.int32, (C, C), 1)
    same_block = _idiv_pow2(row, SOLVE_BLOCK) == _idiv_pow2(col, SOLVE_BLOCK)
    d = [jnp.where(same_block, a, 0.0) for a in a_list]
    e = [a - dd for a, dd in zip(a_list, d)]
    p = [-dd for dd in d]
    td = list(p)
    for _ in range(3):
        p = [_dot(_b16(x), _b16(x)) for x in p]
        td = [t + x + _dot(_b16(t), _b16(x)) for t, x in zip(td, p)]
    f = [ee + _dot(_b16(t), _b16(ee)) for t, ee in zip(td, e)]
    f2 = [_dot(_b16(x), _b16(x)) for x in f]
    f4 = [_dot(_b16(x), _b16(x)) for x in f2]
    h1 = [x2 - x - _dot(_b16(x), _b16(x2)) for x, x2 in zip(f, f2)]
    gd = [h + x4 + _dot(_b16(h), _b16(x4)) for h, x4 in zip(h1, f4)]
    return [g + t + _dot(_b16(g), _b16(t)) for g, t in zip(gd, td)]


def _delta_prompt_kernel(q_ref, k_ref, v_ref, z_ref, ba_ref, cwq_ref, cwk_ref, cwv_ref,
                         c0q_ref, c0k_ref, c0v_ref, s0_ref, alog_ref, dtb_ref, dn_ref, ymix_any,
                         y_ref, sfin_ref, cnq_ref, cnk_ref, cnv_ref,
                         s_scr, xbuf, ybuf, *, n_heads, lc, batch):
    del ymix_any
    h = pl.program_id(0)
    c = pl.program_id(1)
    nc = pl.num_programs(1)
    C = DELTA_CHUNK
    B = batch
    n_chunks = lc // C
    PAD = SUBLANES

    def rows_of(ref, b):
        return ref[pl.ds(b, lc, stride=B), :]

    @pl.when(c == 0)
    def _init():
        s_scr[...] = s0_ref[...]
        for b in range(B):
            xbuf[0, b, PAD - 3:PAD, :] = c0q_ref[b]
            xbuf[1, b, PAD - 3:PAD, :] = c0k_ref[b]
            xbuf[2, b, PAD - 3:PAD, :] = c0v_ref[b]

    def conv_silu(j, b, x_ref, cw_ref, cn_ref):
        xbuf[j, b, PAD:PAD + lc, :] = rows_of(x_ref, b)
        cw = cw_ref[...]
        acc = xbuf[j, b, PAD - 3:PAD - 3 + lc, :] * cw[0:1, :]
        acc = acc + xbuf[j, b, PAD - 2:PAD - 2 + lc, :] * cw[1:2, :]
        acc = acc + xbuf[j, b, PAD - 1:PAD - 1 + lc, :] * cw[2:3, :]
        acc = acc + xbuf[j, b, PAD:PAD + lc, :] * cw[3:4, :]
        tail = xbuf[j, b, PAD + lc - 3:PAD + lc, :]
        xbuf[j, b, PAD - 3:PAD, :] = tail

        @pl.when(c == nc - 1)
        def _():
            cn_ref[b] = tail

        return _silu(acc)

    a_log = _lane_pick(alog_ref[...], h)
    dt_b = _lane_pick(dtb_ref[...], h)
    neg_a = -jnp.exp(a_log)

    crow = lax.broadcasted_iota(jnp.int32, (C, C), 0)
    ccol = lax.broadcasted_iota(jnp.int32, (C, C), 1)
    lane = lax.broadcasted_iota(jnp.int32, (C, LANES), 1)

    qs, ks, vs, betas, gmat = [], [], [], [], jnp.zeros((C, LANES), F32)
    for b in range(B):
        q = conv_silu(0, b, q_ref, cwq_ref, cnq_ref)
        k = conv_silu(1, b, k_ref, cwk_ref, cnk_ref)
        v = conv_silu(2, b, v_ref, cwv_ref, cnv_ref)
        q = q * lax.rsqrt(jnp.sum(q * q, axis=-1, keepdims=True) + EPS) * (HEAD_DIM ** -0.5)
        k = k * lax.rsqrt(jnp.sum(k * k, axis=-1, keepdims=True) + EPS)
        ba = rows_of(ba_ref, b)
        beta = _sigmoid(_lane_pick(ba, h))
        g = neg_a * _softplus(_lane_pick(ba, h + n_heads) + dt_b)
        for ci in range(n_chunks):
            sl = slice(ci * C, (ci + 1) * C)
            u = b * n_chunks + ci
            qs.append(q[sl]); ks.append(k[sl]); vs.append(v[sl]); betas.append(beta[sl])
            gmat = gmat + jnp.where(lane == u, g[sl], 0.0)
    n_units = B * n_chunks
    assert n_units <= LANES

    tri = jnp.where(crow >= ccol, 1.0, 0.0).astype(F32)
    gc_mat = _dot(tri, gmat, HIGHEST)
    gc_t = jnp.transpose(gc_mat)

    lower_incl = crow >= ccol
    lower_strict = crow > ccol
    gcs = [gc_mat[:, u:u + 1] for u in range(n_units)]
    egc = [jnp.exp(x) for x in gcs]
    kb = [k_ * b_ for k_, b_ in zip(ks, betas)]
    k16 = [_b16(x) for x in ks]
    kq = [_dot_nt(_b16(jnp.concatenate([kb_, q_], axis=0)), k_) for kb_, q_, k_ in zip(kb, qs, k16)]
    decay_incl = [jnp.exp(jnp.where(lower_incl, gcs[u] - gc_t[u:u + 1, :], -jnp.inf))
                  for u in range(n_units)]
    a_strict = [jnp.where(lower_strict, x[:C] * d_, 0.0) for x, d_ in zip(kq, decay_incl)]
    attn16 = [_b16(x[C:] * d_) for x, d_ in zip(kq, decay_incl)]
    md = _unit_lower_inverse_minus_eye(a_strict)
    rhs = [jnp.concatenate([v_ * b_, kb_ * e_], axis=-1)
           for v_, b_, kb_, e_ in zip(vs, betas, kb, egc)]
    sol = [r + _dot(_b16(m), _b16(r)) for m, r in zip(md, rhs)]
    gc_last = [x[C - 1:C, :] for x in gcs]
    kcum_qdec16 = [_b16(jnp.concatenate([s_[:, HEAD_DIM:], q_ * e_], axis=0))
                   for s_, q_, e_ in zip(sol, qs, egc)]
    kdec16 = [_b16(k_ * jnp.exp(gl - g_)) for k_, gl, g_ in zip(ks, gc_last, gcs)]
    g_last = [jnp.exp(x) for x in gc_last]

    s = [s_scr[b] for b in range(B)]
    o_units = [None] * n_units
    for ci in range(n_chunks):
        us = [b * n_chunks + ci for b in range(B)]
        ks_qs = [_dot(kcum_qdec16[u], _b16(s[b])) for b, u in enumerate(us)]
        v_new16 = [_b16(sol[u][:, :HEAD_DIM] - x[:C]) for u, x in zip(us, ks_qs)]
        for b, u in enumerate(us):
            o_units[u] = ks_qs[b][C:] + _dot(attn16[u], v_new16[b])
            s[b] = s[b] * g_last[u] + _dot_tn(kdec16[u], v_new16[b])
    for b in range(B):
        s_scr[b] = s[b]

    @pl.when(c == nc - 1)
    def _():
        for b in range(B):
            sfin_ref[b] = s[b]

    dn = dn_ref[...]
    for b in range(B):
        o = jnp.concatenate(o_units[b * n_chunks:(b + 1) * n_chunks], axis=0)
        o = o * lax.rsqrt(jnp.mean(o * o, axis=-1, keepdims=True) + EPS) * dn
        ybuf[pl.ds(b, lc, stride=B), :] = o * _silu(rows_of(z_ref, b))
    y_ref[...] = ybuf[...].astype(y_ref.dtype)


def delta_prompt(qkvz, uba, conv_w, conv0, s0, alog_pad, dtb_pad, d_norm, ymix, *,
                 seq, batch, n_heads):
    H = n_heads
    B = batch
    lc = _pick_tile(seq, 512, DELTA_CHUNK)
    assert lc % DELTA_CHUNK == 0
    nblk = seq // lc
    gate_block = uba.shape[1] // LANES - GATE_TILES
    rb = lc * B

    def col_spec(part):
        return pl.BlockSpec((rb, LANES), lambda h, c: (c, part * H + h))

    def cw_spec(part):
        return pl.BlockSpec((4, LANES), lambda h, c: (0, part * H + h))

    def c0_spec(part):
        return pl.BlockSpec((B, 3, LANES), lambda h, c: (0, 0, part * H + h))

    pad_spec = pl.BlockSpec((1, LANES), lambda h, c: (0, 0))
    state_spec = pl.BlockSpec((B, None, HEAD_DIM, HEAD_DIM), lambda h, c: (0, h, 0, 0))
    in_specs = [col_spec(0), col_spec(1), col_spec(2), col_spec(3),
                pl.BlockSpec((rb, LANES), lambda h, c: (c, gate_block)),
                cw_spec(0), cw_spec(1), cw_spec(2),
                c0_spec(0), c0_spec(1), c0_spec(2),
                state_spec, pad_spec, pad_spec, pad_spec,
                pl.BlockSpec(memory_space=pl.ANY)]
    cn_spec = pl.BlockSpec((B, 3, LANES), lambda h, c: (0, 0, h))
    out_specs = [pl.BlockSpec((rb, LANES), lambda h, c: (c, h)), state_spec, cn_spec, cn_spec, cn_spec]
    cn_shape = jax.ShapeDtypeStruct((B, 3, H * HEAD_DIM), F32)
    out_shape = [jax.ShapeDtypeStruct(ymix.shape, ymix.dtype),
                 jax.ShapeDtypeStruct((B, H, HEAD_DIM, HEAD_DIM), F32),
                 cn_shape, cn_shape, cn_shape]
    kern = functools.partial(_delta_prompt_kernel, n_heads=H, lc=lc, batch=B)
    return pl.pallas_call(
        kern,
        grid=(H, nblk),
        in_specs=in_specs,
        out_specs=out_specs,
        out_shape=out_shape,
        scratch_shapes=[pltpu.VMEM((B, HEAD_DIM, HEAD_DIM), F32),
                        pltpu.VMEM((3, B, SUBLANES + lc, LANES), F32),
                        pltpu.VMEM((rb, LANES), F32)],
        input_output_aliases={15: 0},
        compiler_params=_cparams(2),
        name="delta_prompt",
    )(qkvz, qkvz, qkvz, qkvz, uba, conv_w, conv_w, conv_w,
      conv0, conv0, conv0, s0, alog_pad, dtb_pad, d_norm, ymix)


def _delta_decode_kernel(q_ref, k_ref, v_ref, z_ref, ba_ref, cw_ref, c0_ref, s0_ref,
                         alog_ref, dtb_ref, dn_ref, *rest, n_heads, bb, has_prev):
    n_any = 2 if has_prev else 1
    y_ref, snew_ref, cnew_ref, qn_scr, kn_scr, vn_scr, o_scr, bg_scr = rest[n_any:]
    H = n_heads
    W = H * HEAD_DIM
    cw = cw_ref[...]

    def conv_silu(part, x_ref):
        x = x_ref[...]
        sl = slice(part * W, (part + 1) * W)
        acc = c0_ref[0, :, sl] * cw[0:1, sl]
        acc = acc + c0_ref[1, :, sl] * cw[1:2, sl]
        acc = acc + c0_ref[2, :, sl] * cw[2:3, sl]
        acc = acc + x * cw[3:4, sl]
        cnew_ref[0, :, sl] = c0_ref[1, :, sl]
        cnew_ref[1, :, sl] = c0_ref[2, :, sl]
        cnew_ref[2, :, sl] = x
        return _silu(acc)

    q = conv_silu(0, q_ref)
    k = conv_silu(1, k_ref)
    v = conv_silu(2, v_ref)
    for h in range(H):
        hs = slice(h * HEAD_DIM, (h + 1) * HEAD_DIM)
        qh, kh = q[:, hs], k[:, hs]
        qn_scr[h] = qh * lax.rsqrt(jnp.sum(qh * qh, axis=-1, keepdims=True) + EPS) * (HEAD_DIM ** -0.5)
        kn_scr[h] = kh * lax.rsqrt(jnp.sum(kh * kh, axis=-1, keepdims=True) + EPS)
        vn_scr[h] = v[:, hs]

    ba = ba_ref[...]
    lane = lax.broadcasted_iota(jnp.int32, ba.shape, 1)
    beta_all = _sigmoid(ba)
    g_all = -jnp.exp(alog_ref[...]) * _softplus(ba + dtb_ref[...])
    bg_scr[...] = jnp.where(lane < H, beta_all, g_all)

    ROWS16 = 2 * SUBLANES

    def split(x):
        hi = _b16(x).astype(F32)
        return hi, _b16(x - hi).astype(F32)

    def stack16(rows):
        pad = jnp.zeros((ROWS16 - len(rows), HEAD_DIM), F32)
        return _b16(jnp.concatenate(list(rows) + [pad], axis=0))

    def per_seq(bi, carry):
        bg = bg_scr[pl.ds(bi, 1), :]
        heads = range(H)
        k_row = [kn_scr[h, pl.ds(bi, 1), :] for h in heads]
        q_row = [qn_scr[h, pl.ds(bi, 1), :] for h in heads]
        v_row = [vn_scr[h, pl.ds(bi, 1), :] for h in heads]
        beta = [bg[:, h:h + 1] for h in heads]
        eg = [jnp.exp(bg[:, H + h:H + h + 1]) for h in heads]
        s = [s0_ref[bi, h] for h in heads]
        k_sp = [split(x) for x in k_row]
        q_sp = [split(x) for x in q_row]
        s_sp = [split(x) for x in s]
        lhs = [stack16([kh, kl, qh, ql]) for (kh, kl), (qh, ql) in zip(k_sp, q_sp)]
        r_hi = [_dot(l_, _b16(sh)) for l_, (sh, _) in zip(lhs, s_sp)]
        r_lo = [_dot(l_, _b16(sl)) for l_, (_, sl) in zip(lhs, s_sp)]
        ks = [a[0:1] + a[1:2] + b[0:1] for a, b in zip(r_hi, r_lo)]
        qs = [a[2:3] + a[3:4] + b[2:3] for a, b in zip(r_hi, r_lo)]
        v_new = [b_ * v_ - (b_ * e_) * ks_ for b_, v_, e_, ks_ in zip(beta, v_row, eg, ks)]
        qk = [jnp.sum(q_ * k_, axis=-1, keepdims=True) for q_, k_ in zip(q_row, k_row)]
        o = [e_ * qs_ + qk_ * vn_ for e_, qs_, qk_, vn_ in zip(eg, qs, qk, v_new)]
        vn_sp = [split(x) for x in v_new]
        outer = [_dot_tn(stack16([kh, kh, kl]), stack16([vh, vl, vh]))
                 for (kh, kl), (vh, vl) in zip(k_sp, vn_sp)]
        for h in heads:
            snew_ref[bi, h] = s[h] * eg[h] + outer[h]
            o_scr[h, pl.ds(bi, 1), :] = o[h]
        return carry

    lax.fori_loop(0, bb, per_seq, 0)

    dn = dn_ref[...]
    z = z_ref[...]
    for h in range(H):
        hs = slice(h * HEAD_DIM, (h + 1) * HEAD_DIM)
        o = o_scr[h]
        o = o * lax.rsqrt(jnp.mean(o * o, axis=-1, keepdims=True) + EPS) * dn
        y_ref[:, hs] = (o * _silu(z[:, hs])).astype(y_ref.dtype)


def delta_decode(qkvz, uba, conv_w, conv0_t, s0_all, layer, alog_pad2, dtb_pad2, d_norm, ymix,
                 snew_all, *, row_off, n_seq, n_heads):
    H = n_heads
    W = H * HEAD_DIM
    bb = SUBLANES
    assert n_seq % bb == 0 and row_off % bb == 0
    ro = row_off // bb
    gate_block = uba.shape[1] // LANES - GATE_TILES

    def col_spec(part):
        return pl.BlockSpec((bb, W), lambda i: (ro + i, part))

    pad_spec = pl.BlockSpec((1, LANES), lambda i: (0, 0))
    state_spec = pl.BlockSpec((None, bb, H, HEAD_DIM, HEAD_DIM), lambda i: (layer, i, 0, 0, 0))
    in_specs = [col_spec(0), col_spec(1), col_spec(2), col_spec(3),
                pl.BlockSpec((bb, LANES), lambda i: (ro + i, gate_block)),
                pl.BlockSpec((4, 3 * W), lambda i: (0, 0)),
                pl.BlockSpec((3, bb, 3 * W), lambda i: (0, i, 0)),
                state_spec,
                pad_spec, pad_spec, pad_spec,
                pl.BlockSpec(memory_space=pl.ANY)]
    args = [qkvz, qkvz, qkvz, qkvz, uba, conv_w, conv0_t, s0_all, alog_pad2, dtb_pad2, d_norm, ymix]
    aliases = {11: 0}
    if snew_all is not None:
        in_specs.append(pl.BlockSpec(memory_space=pl.ANY))
        args.append(snew_all)
        aliases[12] = 1
    out_specs = [pl.BlockSpec((bb, W), lambda i: (ro + i, 0)),
                 state_spec,
                 pl.BlockSpec((3, bb, 3 * W), lambda i: (0, i, 0))]
    out_shape = [jax.ShapeDtypeStruct(ymix.shape, ymix.dtype),
                 jax.ShapeDtypeStruct(s0_all.shape, F32),
                 jax.ShapeDtypeStruct((3, n_seq, 3 * W), F32)]
    kern = functools.partial(_delta_decode_kernel, n_heads=H, bb=bb, has_prev=snew_all is not None)
    head_scr = pltpu.VMEM((H, bb, HEAD_DIM), F32)
    return pl.pallas_call(
        kern,
        grid=(n_seq // bb,),
        in_specs=in_specs,
        out_specs=out_specs,
        out_shape=out_shape,
        scratch_shapes=[head_scr, head_scr, head_scr, head_scr, pltpu.VMEM((bb, LANES), F32)],
        input_output_aliases=aliases,
        compiler_params=_cparams(1),
        name="delta_decode",
    )(*args)


def _s5_prep_kernel(are_ref, aim_ref, ls_ref, bre_ref, bim_ref,
                    lbre_ref, lbim_ref, bbre_ref, bbim_ref, *, n_state, group):
    a_re = are_ref[...]
    a_im = aim_ref[...]
    dt = jnp.exp(ls_ref[...])
    mag = jnp.exp(a_re * dt)
    ang = a_im * dt
    lb_re = mag * jnp.cos(ang)
    lb_im = mag * jnp.sin(ang)
    den = a_re * a_re + a_im * a_im
    num_re = lb_re - 1.0
    num_im = lb_im
    f_re = (num_re * a_re + num_im * a_im) / den
    f_im = (num_im * a_re - num_re * a_im) / den
    lbre_ref[...] = lb_re
    lbim_ref[...] = lb_im
    r = lax.broadcasted_iota(jnp.int32, (n_state, n_state * group), 0)
    cidx = lax.broadcasted_iota(jnp.int32, (n_state, n_state * group), 1)
    expand = jnp.where(_idiv_pow2(cidx, group) == r, 1.0, 0.0).astype(F32)
    fre_x = _dot(f_re, expand, HIGHEST)
    fim_x = _dot(f_im, expand, HIGHEST)
    b_re = bre_ref[...]
    b_im = bim_ref[...]
    bbre_ref[...] = fre_x * b_re - fim_x * b_im
    bbim_ref[...] = fre_x * b_im + fim_x * b_re


def s5_prep(a_re, a_im, log_step, b_re, b_im):
    g, n = a_re.shape
    group = b_re.shape[-1]
    full = lambda shape: pl.BlockSpec(shape, lambda: (0,) * len(shape))
    kern = functools.partial(_s5_prep_kernel, n_state=n, group=group)
    lb_re, lb_im, bb_re, bb_im = pl.pallas_call(
        kern,
        in_specs=[full((g, n)), full((g, n)), full((g, 1)), full((g, n * group)), full((g, n * group))],
        out_specs=[full((g, n)), full((g, n)), full((g, n * group)), full((g, n * group))],
        out_shape=[jax.ShapeDtypeStruct((g, n), F32), jax.ShapeDtypeStruct((g, n), F32),
                   jax.ShapeDtypeStruct((g, n * group), F32), jax.ShapeDtypeStruct((g, n * group), F32)],
        compiler_params=pltpu.CompilerParams(vmem_limit_bytes=VMEM_LIMIT_BYTES),
        name="s5_prep",
    )(a_re, a_im, log_step.reshape(g, 1), b_re.reshape(g, n * group), b_im.reshape(g, n * group))
    return lb_re, lb_im, bb_re.reshape(g, n, group), bb_im.reshape(g, n, group)


def _s5_kernel(u_ref, h0r_ref, h0i_ref, lbr_ref, lbi_ref, wbr_ref, wbi_ref, wc_ref,
               d_ref, wg_ref, bg_ref, ymix_any,
               y_ref, hr_out, hi_out,
               xr_scr, xi_scr, hr_scr, hi_scr, yc_scr, *, rb, single_step, n_blocks, blk_in, blk_state):
    del ymix_any
    t_blk = pl.program_id(1)
    nt = pl.num_programs(1)
    R = SUBLANES
    half = R // 2

    if not single_step:
        @pl.when(t_blk == 0)
        def _init():
            hr_scr[...] = h0r_ref[...]
            hi_scr[...] = h0i_ref[...]

    u = u_ref[...]
    u16 = u.astype(BF16)
    for j in range(n_blocks):
        uj = u16[:, j * blk_in:(j + 1) * blk_in]
        xr_scr[:, j * blk_state:(j + 1) * blk_state] = _dot(uj, wbr_ref[j])
        xi_scr[:, j * blk_state:(j + 1) * blk_state] = _dot(uj, wbi_ref[j])

    n_lanes = n_blocks * blk_state
    lane_chunk = _pick_tile(n_lanes, 1024, LANES)
    low = lax.broadcasted_iota(jnp.int32, (R, lane_chunk), 0) < half
    for lc in range(n_lanes // lane_chunk):
        ls = slice(lc * lane_chunk, (lc + 1) * lane_chunk)
        if single_step:
            lbr, lbi = lbr_ref[:, ls], lbi_ref[:, ls]
            hr, hi = h0r_ref[:, ls], h0i_ref[:, ls]
            xr = lbr * hr - lbi * hi + xr_scr[:, ls]
            xi = lbr * hi + lbi * hr + xi_scr[:, ls]
            xr_scr[:, ls] = xr
            xi_scr[:, ls] = xi
            hr_out[:, ls] = xr
            hi_out[:, ls] = xi
            continue
        lbr = jnp.broadcast_to(lbr_ref[:, ls], (R, lane_chunk))
        lbi = jnp.broadcast_to(lbi_ref[:, ls], (R, lane_chunk))

        def step4(t, carry):
            hr, hi = carry
            rows = pl.ds(pl.multiple_of(t * R, R), R)
            br, bi = xr_scr[rows, ls], xi_scr[rows, ls]
            pr, pi = pltpu.roll(hr, half, 0), pltpu.roll(hi, half, 0)
            y1r = lbr * pr - lbi * pi + br
            y1i = lbr * pi + lbi * pr + bi
            pr, pi = pltpu.roll(y1r, half, 0), pltpu.roll(y1i, half, 0)
            y2r = lbr * pr - lbi * pi + br
            y2i = lbr * pi + lbi * pr + bi
            xr = jnp.where(low, y1r, y2r)
            xi = jnp.where(low, y1i, y2i)
            xr_scr[rows, ls] = xr
            xi_scr[rows, ls] = xi
            return xr, xi

        hr, hi = lax.fori_loop(0, rb // R, step4, (hr_scr[:, ls], hi_scr[:, ls]))
        hr_scr[:, ls] = hr
        hi_scr[:, ls] = hi

    if not single_step:
        @pl.when(t_blk == nt - 1)
        def _fin():
            hr_out[...] = hr_scr[...]
            hi_out[...] = hi_scr[...]

    for j in range(n_blocks):
        ss = slice(j * blk_state, (j + 1) * blk_state)
        yj = _dot(xr_scr[:, ss].astype(BF16), wc_ref[j, :blk_state, :])
        yj = yj + _dot(xi_scr[:, ss].astype(BF16), wc_ref[j, blk_state:, :])
        yc_scr[:, j * blk_in:(j + 1) * blk_in] = yj
    y = _gelu_tanh(yc_scr[...] + d_ref[...] * u)
    gate = _dot(y.astype(BF16), wg_ref[...]) + bg_ref[...]
    y_ref[...] = (y * _sigmoid(gate)).astype(y_ref.dtype)


def s5_mixer(uba, h0_re, h0_im, lb_re, lb_im, wb_re, wb_im, wc, d, w_glu16, b_glu, ymix, *,
             row_off, n_groups, n_time_blocks, rb, single_step, width, col_block):
    n_blocks, blk_in, blk_state = wb_re.shape
    n_lanes = n_blocks * blk_state
    assert row_off % rb == 0 and rb % (2 * SUBLANES) == 0
    assert n_time_blocks == 1 if single_step else n_groups == 1
    ro = row_off // rb

    def const(shape):
        return pl.BlockSpec(shape, lambda s, t: (0,) * len(shape))

    state_spec = pl.BlockSpec((rb if single_step else SUBLANES, n_lanes), lambda s, t: (s, 0))
    in_specs = [pl.BlockSpec((rb, width), lambda s, t: (ro + s * n_time_blocks + t, 0)),
                state_spec, state_spec,
                const((1, n_lanes)), const((1, n_lanes)),
                const(wb_re.shape), const(wb_im.shape), const(wc.shape),
                const((1, width)), const(w_glu16.shape), const((1, width)),
                pl.BlockSpec(memory_space=pl.ANY)]
    out_specs = [pl.BlockSpec((rb, width), lambda s, t: (ro + s * n_time_blocks + t, col_block)),
                 state_spec, state_spec]
    out_shape = [jax.ShapeDtypeStruct(ymix.shape, ymix.dtype),
                 jax.ShapeDtypeStruct(h0_re.shape, F32),
                 jax.ShapeDtypeStruct(h0_im.shape, F32)]
    kern = functools.partial(_s5_kernel, rb=rb, single_step=single_step, n_blocks=n_blocks, blk_in=blk_in,
                             blk_state=blk_state)
    return pl.pallas_call(
        kern,
        grid=(n_groups, n_time_blocks),
        in_specs=in_specs,
        out_specs=out_specs,
        out_shape=out_shape,
        scratch_shapes=[pltpu.VMEM((rb, n_lanes), F32), pltpu.VMEM((rb, n_lanes), F32),
                        pltpu.VMEM((SUBLANES, n_lanes), F32), pltpu.VMEM((SUBLANES, n_lanes), F32),
                        pltpu.VMEM((rb, width), F32)],
        input_output_aliases={11: 0},
        compiler_params=_cparams(2),
        name="s5_mixer",
    )(uba, h0_re, h0_im, lb_re.reshape(1, n_lanes), lb_im.reshape(1, n_lanes),
      wb_re, wb_im, wc, d.reshape(1, width), w_glu16, b_glu.reshape(1, width), ymix)


def _s5_block_weights(bb_re, bb_im, c_re, c_im):
    g, n, grp = bb_re.shape
    r = S5_GROUPS_PER_BLOCK
    nb = g // r
    eye = jnp.eye(r, dtype=F32)

    def in_blocks(bb):
        w = jnp.einsum("jsnc,rs->jrcsn", bb.reshape(nb, r, n, grp), eye)
        return w.reshape(nb, r * grp, r * n).astype(BF16)

    def out_blocks(cc):
        w = jnp.einsum("jrcn,rs->jsnrc", cc.reshape(nb, r, grp, n), eye)
        return w.reshape(nb, r * n, r * grp)

    wc = jnp.concatenate([out_blocks(c_re), -out_blocks(c_im)], axis=1).astype(BF16)
    return in_blocks(bb_re), in_blocks(bb_im), wc


def kernel(x_prompt, x_sample, state_delta, state_conv, state_s5_re, state_s5_im, norm_mix, w_in, conv_w, delta_a_log, delta_dt_bias, delta_norm, s5_a_re, s5_a_im, s5_log_step, s5_b_re, s5_b_im, s5_c_re, s5_c_im, s5_d, s5_w_glu, s5_b_glu, w_out, norm_ffn, w_ffn_in, w_ffn_out, norm_final):
    B, L, D = x_prompt.shape
    Bs = x_sample.shape[0]
    depth = w_in.shape[0]
    H = delta_a_log.shape[1]
    Wd = H * HEAD_DIM
    G, N = s5_a_re.shape[1:]
    grp = s5_b_re.shape[-1]
    Ws = G * grp
    d_ff = w_ffn_out.shape[1]
    Mp = B * L
    M = Mp + Bs
    half = SUBLANES // 2
    assert B == half and x_sample.shape[1] == 1 and Wd % Ws == 0
    uba_w = Ws + GATE_TILES * LANES

    x = jnp.concatenate([jnp.transpose(x_prompt, (1, 0, 2)).reshape(Mp, D),
                         x_sample.reshape(Bs, D)], axis=0)

    gate_cols = 4 * Wd
    w_uba = jnp.concatenate([w_in[:, :, gate_cols + 2 * H:], w_in[:, :, gate_cols:gate_cols + 2 * H],
                             jnp.zeros((depth, D, GATE_TILES * LANES - 2 * H), F32)], axis=2)
    pad_h = lambda a: jnp.pad(a, (0, LANES - H)).reshape(1, LANES)
    pad_h2 = lambda a: jnp.pad(a, (H, LANES - 2 * H)).reshape(1, LANES)
    zero_s = jnp.zeros((B, H, HEAD_DIM, HEAD_DIM), F32)
    zero_c = jnp.zeros((B, 3, 3 * Wd), F32)
    zero_h = jnp.zeros((SUBLANES, G * N), F32)
    s5_rb = _pick_tile(Mp, 128, 2 * SUBLANES)
    s5_rb_dec = _pick_tile(Bs, 32, 2 * SUBLANES)

    w_ffn_out16 = cast_bf16(w_ffn_out)
    s_dec = None
    outs = {k: [] for k in ("dp", "cp", "rp", "ip", "cs", "rs", "is")}
    for l in range(depth):
        hmix = rmsnorm_rows(x, norm_mix[l], BF16)
        qkvz = matmul(hmix, w_in, l, n=gate_cols)
        uba = matmul(hmix, w_uba, l, n=uba_w)

        ymix = jnp.zeros((M, Wd + Ws), BF16)
        dnorm = delta_norm[l].reshape(1, HEAD_DIM)
        ymix, s_p, cnq, cnk, cnv = delta_prompt(
            qkvz, uba, conv_w[l], zero_c, zero_s, pad_h(delta_a_log[l]), pad_h(delta_dt_bias[l]),
            dnorm, ymix, seq=L, batch=B, n_heads=H)
        ymix, s_dec, cn_s = delta_decode(
            qkvz, uba, conv_w[l], jnp.transpose(state_conv[l], (1, 0, 2)), state_delta, l,
            pad_h2(delta_a_log[l]), pad_h2(delta_dt_bias[l]), dnorm, ymix, s_dec,
            row_off=Mp, n_seq=Bs, n_heads=H)
        outs["dp"].append(s_p)
        outs["cp"].append(jnp.concatenate([cnq, cnk, cnv], axis=-1))
        outs["cs"].append(jnp.transpose(cn_s, (1, 0, 2)))

        lb_re, lb_im, bb_re, bb_im = s5_prep(s5_a_re[l], s5_a_im[l], s5_log_step[l],
                                             s5_b_re[l], s5_b_im[l])
        wb_re, wb_im, wc = _s5_block_weights(bb_re, bb_im, s5_c_re[l], s5_c_im[l])
        w_glu16 = s5_w_glu[l].astype(BF16)
        ymix, hr_p, hi_p = s5_mixer(
            uba, zero_h, zero_h, lb_re, lb_im, wb_re, wb_im, wc, s5_d[l], w_glu16, s5_b_glu[l], ymix,
            row_off=0, n_groups=1, n_time_blocks=Mp // s5_rb, rb=s5_rb, single_step=False, width=Ws,
            col_block=Wd // Ws)
        ymix, hr_s, hi_s = s5_mixer(
            uba, state_s5_re[l].reshape(Bs, G * N), state_s5_im[l].reshape(Bs, G * N),
            lb_re, lb_im, wb_re, wb_im, wc, s5_d[l], w_glu16, s5_b_glu[l], ymix,
            row_off=Mp, n_groups=Bs // s5_rb_dec, n_time_blocks=1, rb=s5_rb_dec, single_step=True,
            width=Ws, col_block=Wd // Ws)
        outs["rp"].append(hr_p[half:].reshape(B, G, N))
        outs["ip"].append(hi_p[half:].reshape(B, G, N))
        outs["rs"].append(hr_s.reshape(Bs, G, N))
        outs["is"].append(hi_s.reshape(Bs, G, N))

        x = matmul(ymix, w_out, l, n=D, res=x)

        hffn = rmsnorm_rows(x, norm_ffn[l], BF16)
        hidden = matmul(hffn, w_ffn_in, l, n=d_ff, col_off2=d_ff, out_dtype=BF16, tn=256)
        x = matmul(hidden, w_ffn_out16, l, n=D, res=x, tm=520, tn=256, rows_outer=True)

    y = rmsnorm_rows(x, norm_final, F32)
    y_prompt = jnp.transpose(y[:Mp].reshape(L, B, D), (1, 0, 2))
    y_sample = y[Mp:].reshape(Bs, 1, D)
    st = lambda k: jnp.stack(outs[k])
    return (y_prompt, y_sample, st("dp"), st("cp"), st("rp"), st("ip"),
            s_dec, st("cs"), st("rs"), st("is"))
```

```python
import functools

import jax
import jax.numpy as jnp
from jax import lax
from jax.experimental import pallas as pl
from jax.experimental.pallas import tpu as pltpu

F32 = jnp.float32
BF16 = jnp.bfloat16
EPS = 1e-6

LANES = 128
SUBLANES = 8
VMEM_LIMIT_BYTES = 56 * 1024 * 1024
HEAD_DIM = 128
DELTA_CHUNK = 128
SOLVE_BLOCK = 16
S5_GROUPS_PER_BLOCK = 8
GATE_TILES = 4
HIGHEST = lax.Precision.HIGHEST


def _cparams(n_axes):
    return pltpu.CompilerParams(dimension_semantics=("arbitrary",) * n_axes,
                                vmem_limit_bytes=VMEM_LIMIT_BYTES)


def _pick_tile(n, target, mult):
    best = None
    for d in range(mult, min(n, target) + 1, mult):
        if n % d == 0:
            best = d
    return n if best is None else best


def _idiv_pow2(x, c):
    shift = c.bit_length() - 1
    assert c == 1 << shift
    return lax.shift_right_logical(x, jnp.int32(shift))


def _sigmoid(x):
    return 1.0 / (1.0 + jnp.exp(-x))


def _silu(x):
    return x * _sigmoid(x)


def _softplus(x):
    return jnp.maximum(x, 0.0) + jnp.log(1.0 + jnp.exp(-jnp.abs(x)))


def _gelu_tanh(x):
    c = 0.7978845608028654
    return 0.5 * x * (1.0 + jnp.tanh(c * (x + 0.044715 * (x * x * x))))


def _dot(a, b, precision=None):
    return jnp.dot(a, b, preferred_element_type=F32, precision=precision)


def _dot_nt(a, b, precision=None):
    return lax.dot_general(a, b, (((1,), (1,)), ((), ())), preferred_element_type=F32,
                           precision=precision)


def _dot_tn(a, b, precision=None):
    return lax.dot_general(a, b, (((0,), (0,)), ((), ())), preferred_element_type=F32,
                           precision=precision)


def _b16(x):
    return x.astype(BF16)


def _rmsnorm_kernel(x_ref, w_ref, o_ref):
    x = x_ref[...]
    y = x * lax.rsqrt(jnp.mean(x * x, axis=-1, keepdims=True) + EPS)
    o_ref[...] = (y * w_ref[...]).astype(o_ref.dtype)


def rmsnorm_rows(x, w, out_dtype):
    m, d = x.shape
    tr = _pick_tile(m, 512, 16)
    return pl.pallas_call(
        _rmsnorm_kernel,
        grid=(m // tr,),
        in_specs=[pl.BlockSpec((tr, d), lambda i: (i, 0)),
                  pl.BlockSpec((1, d), lambda i: (0, 0))],
        out_specs=pl.BlockSpec((tr, d), lambda i: (i, 0)),
        out_shape=jax.ShapeDtypeStruct((m, d), out_dtype),
        compiler_params=_cparams(1),
        name="rmsnorm_rows",
    )(x, w.reshape(1, d))


def _mm_kernel(*refs, n_w, has_res, k_total, tk):
    x_ref = refs[0]
    w_refs = refs[1:1 + n_w]
    pos = 1 + n_w
    res_ref = refs[pos] if has_res else None
    pos += int(has_res)
    o_ref = refs[pos]
    acc_refs = refs[pos + 1:]
    k = pl.program_id(2)
    nk = pl.num_programs(2)
    single_k = tk == k_total

    def finish(parts):
        y = parts[0]
        if n_w == 2:
            y = _silu(y) * parts[1]
        if has_res:
            y = y + res_ref[...]
        o_ref[...] = y.astype(o_ref.dtype)

    x = x_ref[...]
    ragged = (k_total % tk) != 0
    if ragged:
        valid = k_total - k * tk
        x = jnp.where(lax.broadcasted_iota(jnp.int32, x.shape, 1) < valid, x, jnp.zeros_like(x))
    parts = []
    for w_ref in w_refs:
        w = w_ref[...]
        if ragged:
            w = jnp.where(lax.broadcasted_iota(jnp.int32, w.shape, 0) < valid, w, 0.0)
        parts.append(_dot(x, w.astype(BF16)))
    if single_k:
        finish(parts)
        return

    @pl.when(k == 0)
    def _init():
        for a, p in zip(acc_refs, parts):
            a[...] = p

    @pl.when(k > 0)
    def _accumulate():
        for a, p in zip(acc_refs, parts):
            a[...] += p

    @pl.when(k == nk - 1)
    def _finish():
        finish([a[...] for a in acc_refs])


def matmul(x, w, layer, *, n, col_off=0, col_off2=None, res=None, out_dtype=F32,
           tm=1040, tn=512, tk=None, rows_outer=False, x_single_buffer=False):
    m, k_total = x.shape
    assert w.shape[1] == k_total
    tm = _pick_tile(m, tm, 16)
    tn = _pick_tile(n, tn, LANES)
    tk = k_total if tk is None else tk
    assert col_off % tn == 0 and (col_off2 is None or col_off2 % tn == 0)
    n_w = 1 if col_off2 is None else 2
    nk = pl.cdiv(k_total, tk)
    if rows_outer:
        grid = (m // tm, n // tn, nk)
        ij = lambda a, b: (a, b)
    else:
        grid = (n // tn, m // tm, nk)
        ij = lambda a, b: (b, a)
    x_mode = dict(pipeline_mode=pl.Buffered(1)) if (x_single_buffer and rows_outer and nk == 1) else {}
    in_specs = [pl.BlockSpec((tm, tk), lambda a, b, k: (ij(a, b)[0], k), **x_mode)]
    args = [x]
    for off in (col_off, col_off2)[:n_w]:
        in_specs.append(pl.BlockSpec((None, tk, tn),
                                     lambda a, b, k, o=off // tn: (layer, k, ij(a, b)[1] + o)))
        args.append(w)
    if res is not None:
        in_specs.append(pl.BlockSpec((tm, tn), lambda a, b, k: ij(a, b)))
        args.append(res)
    kern = functools.partial(_mm_kernel, n_w=n_w, has_res=res is not None,
                             k_total=k_total, tk=tk)
    return pl.pallas_call(
        kern,
        grid=grid,
        in_specs=in_specs,
        out_specs=pl.BlockSpec((tm, tn), lambda a, b, k: ij(a, b)),
        out_shape=jax.ShapeDtypeStruct((m, n), out_dtype),
        scratch_shapes=[pltpu.VMEM((tm, tn), F32) for _ in range(n_w if nk > 1 else 0)],
        compiler_params=_cparams(3),
        name="matmul_rows",
    )(*args)


def _cast_kernel(x_ref, o_ref):
    o_ref[...] = x_ref[...].astype(o_ref.dtype)


def cast_bf16(w):
    depth, k, n = w.shape
    tr = _pick_tile(k, 512, 16)
    return pl.pallas_call(
        _cast_kernel,
        grid=(depth, k // tr),
        in_specs=[pl.BlockSpec((None, tr, n), lambda l, i: (l, i, 0))],
        out_specs=pl.BlockSpec((None, tr, n), lambda l, i: (l, i, 0)),
        out_shape=jax.ShapeDtypeStruct(w.shape, BF16),
        compiler_params=_cparams(2),
        name="cast_bf16",
    )(w)


def _lane_pick(x, idx):
    lane = lax.broadcasted_iota(jnp.int32, x.shape, 1)
    return jnp.sum(jnp.where(lane == idx, x, 0.0), axis=-1, keepdims=True)


def _unit_lower_inverse_minus_eye(a_list):
    C = DELTA_CHUNK
    assert C == 8 * SOLVE_BLOCK
    row = lax.broadcasted_iota(jnp.int32, (C, C), 0)
    col = lax.broadcasted_iota(jnp.int32, (C, C), 1)
    same_block = _idiv_pow2(row, SOLVE_BLOCK) == _idiv_pow2(col, SOLVE_BLOCK)
    d = [jnp.where(same_block, a, 0.0) for a in a_list]
    e = [a - dd for a, dd in zip(a_list, d)]
    p = [-dd for dd in d]
    td = list(p)
    for _ in range(3):
        p = [_dot(_b16(x), _b16(x)) for x in p]
        td = [t + x + _dot(_b16(t), _b16(x)) for t, x in zip(td, p)]
    f = [ee + _dot(_b16(t), _b16(ee)) for t, ee in zip(td, e)]
    f2 = [_dot(_b16(x), _b16(x)) for x in f]
    f4 = [_dot(_b16(x), _b16(x)) for x in f2]
    h1 = [x2 - x - _dot(_b16(x), _b16(x2)) for x, x2 in zip(f, f2)]
    gd = [h + x4 + _dot(_b16(h), _b16(x4)) for h, x4 in zip(h1, f4)]
    return [g + t + _dot(_b16(g), _b16(t)) for g, t in zip(gd, td)]


def _delta_prompt_kernel(q_ref, k_ref, v_ref, z_ref, ba_ref, cwq_ref, cwk_ref, cwv_ref,
                         c0q_ref, c0k_ref, c0v_ref, s0_ref, alog_ref, dtb_ref, dn_ref, ymix_any,
                         y_ref, sfin_ref, cnq_ref, cnk_ref, cnv_ref,
                         s_scr, xbuf, ybuf, *, n_heads, lc, batch):
    del ymix_any
    h = pl.program_id(0)
    c = pl.program_id(1)
    nc = pl.num_programs(1)
    C = DELTA_CHUNK
    B = batch
    n_chunks = lc // C
    PAD = SUBLANES

    def rows_of(ref, b):
        return ref[pl.ds(b, lc, stride=B), :]

    @pl.when(c == 0)
    def _init():
        s_scr[...] = s0_ref[...]
        for b in range(B):
            xbuf[0, b, PAD - 3:PAD, :] = c0q_ref[b]
            xbuf[1, b, PAD - 3:PAD, :] = c0k_ref[b]
            xbuf[2, b, PAD - 3:PAD, :] = c0v_ref[b]

    def conv_silu(j, b, x_ref, cw_ref, cn_ref):
        xbuf[j, b, PAD:PAD + lc, :] = rows_of(x_ref, b)
        cw = cw_ref[...]
        acc = xbuf[j, b, PAD - 3:PAD - 3 + lc, :] * cw[0:1, :]
        acc = acc + xbuf[j, b, PAD - 2:PAD - 2 + lc, :] * cw[1:2, :]
        acc = acc + xbuf[j, b, PAD - 1:PAD - 1 + lc, :] * cw[2:3, :]
        acc = acc + xbuf[j, b, PAD:PAD + lc, :] * cw[3:4, :]
        tail = xbuf[j, b, PAD + lc - 3:PAD + lc, :]
        xbuf[j, b, PAD - 3:PAD, :] = tail

        @pl.when(c == nc - 1)
        def _():
            cn_ref[b] = tail

        return _silu(acc)

    a_log = _lane_pick(alog_ref[...], h)
    dt_b = _lane_pick(dtb_ref[...], h)
    neg_a = -jnp.exp(a_log)

    crow = lax.broadcasted_iota(jnp.int32, (C, C), 0)
    ccol = lax.broadcasted_iota(jnp.int32, (C, C), 1)
    lane = lax.broadcasted_iota(jnp.int32, (C, LANES), 1)

    qs, ks, vs, betas, gmat = [], [], [], [], jnp.zeros((C, LANES), F32)
    for b in range(B):
        q = conv_silu(0, b, q_ref, cwq_ref, cnq_ref)
        k = conv_silu(1, b, k_ref, cwk_ref, cnk_ref)
        v = conv_silu(2, b, v_ref, cwv_ref, cnv_ref)
        q = q * lax.rsqrt(jnp.sum(q * q, axis=-1, keepdims=True) + EPS) * (HEAD_DIM ** -0.5)
        k = k * lax.rsqrt(jnp.sum(k * k, axis=-1, keepdims=True) + EPS)
        ba = rows_of(ba_ref, b)
        beta = _sigmoid(_lane_pick(ba, h))
        g = neg_a * _softplus(_lane_pick(ba, h + n_heads) + dt_b)
        for ci in range(n_chunks):
            sl = slice(ci * C, (ci + 1) * C)
            u = b * n_chunks + ci
            qs.append(q[sl]); ks.append(k[sl]); vs.append(v[sl]); betas.append(beta[sl])
            gmat = gmat + jnp.where(lane == u, g[sl], 0.0)
    n_units = B * n_chunks
    assert n_units <= LANES

    tri = jnp.where(crow >= ccol, 1.0, 0.0).astype(F32)
    gc_mat = _dot(tri, gmat, HIGHEST)
    gc_t = jnp.transpose(gc_mat)

    lower_incl = crow >= ccol
    lower_strict = crow > ccol
    gcs = [gc_mat[:, u:u + 1] for u in range(n_units)]
    egc = [jnp.exp(x) for x in gcs]
    kb = [k_ * b_ for k_, b_ in zip(ks, betas)]
    k16 = [_b16(x) for x in ks]
    kq = [_dot_nt(_b16(jnp.concatenate([kb_, q_], axis=0)), k_) for kb_, q_, k_ in zip(kb, qs, k16)]
    decay_incl = [jnp.exp(jnp.where(lower_incl, gcs[u] - gc_t[u:u + 1, :], -jnp.inf))
                  for u in range(n_units)]
    a_strict = [jnp.where(lower_strict, x[:C] * d_, 0.0) for x, d_ in zip(kq, decay_incl)]
    attn16 = [_b16(x[C:] * d_) for x, d_ in zip(kq, decay_incl)]
    md = _unit_lower_inverse_minus_eye(a_strict)
    rhs = [jnp.concatenate([v_ * b_, kb_ * e_], axis=-1)
           for v_, b_, kb_, e_ in zip(vs, betas, kb, egc)]
    sol = [r + _dot(_b16(m), _b16(r)) for m, r in zip(md, rhs)]
    gc_last = [x[C - 1:C, :] for x in gcs]
    kcum_qdec16 = [_b16(jnp.concatenate([s_[:, HEAD_DIM:], q_ * e_], axis=0))
                   for s_, q_, e_ in zip(sol, qs, egc)]
    kdec16 = [_b16(k_ * jnp.exp(gl - g_)) for k_, gl, g_ in zip(ks, gc_last, gcs)]
    g_last = [jnp.exp(x) for x in gc_last]

    s = [s_scr[b] for b in range(B)]
    o_units = [None] * n_units
    for ci in range(n_chunks):
        us = [b * n_chunks + ci for b in range(B)]
        ks_qs = [_dot(kcum_qdec16[u], _b16(s[b])) for b, u in enumerate(us)]
        v_new16 = [_b16(sol[u][:, :HEAD_DIM] - x[:C]) for u, x in zip(us, ks_qs)]
        for b, u in enumerate(us):
            o_units[u] = ks_qs[b][C:] + _dot(attn16[u], v_new16[b])
            s[b] = s[b] * g_last[u] + _dot_tn(kdec16[u], v_new16[b])
    for b in range(B):
        s_scr[b] = s[b]

    @pl.when(c == nc - 1)
    def _():
        for b in range(B):
            sfin_ref[b] = s[b]

    dn = dn_ref[...]
    for b in range(B):
        o = jnp.concatenate(o_units[b * n_chunks:(b + 1) * n_chunks], axis=0)
        o = o * lax.rsqrt(jnp.mean(o * o, axis=-1, keepdims=True) + EPS) * dn
        ybuf[pl.ds(b, lc, stride=B), :] = o * _silu(rows_of(z_ref, b))
    y_ref[...] = ybuf[...].astype(y_ref.dtype)


def delta_prompt(qkvz, uba, conv_w, conv0, s0, alog_pad, dtb_pad, d_norm, ymix, *,
                 seq, batch, n_heads):
    H = n_heads
    B = batch
    lc = _pick_tile(seq, 512, DELTA_CHUNK)
    assert lc % DELTA_CHUNK == 0
    nblk = seq // lc
    gate_block = uba.shape[1] // LANES - GATE_TILES
    rb = lc * B

    def col_spec(part):
        return pl.BlockSpec((rb, LANES), lambda h, c: (c, part * H + h))

    def cw_spec(part):
        return pl.BlockSpec((4, LANES), lambda h, c: (0, part * H + h))

    def c0_spec(part):
        return pl.BlockSpec((B, 3, LANES), lambda h, c: (0, 0, part * H + h))

    pad_spec = pl.BlockSpec((1, LANES), lambda h, c: (0, 0))
    state_spec = pl.BlockSpec((B, None, HEAD_DIM, HEAD_DIM), lambda h, c: (0, h, 0, 0))
    in_specs = [col_spec(0), col_spec(1), col_spec(2), col_spec(3),
                pl.BlockSpec((rb, LANES), lambda h, c: (c, gate_block)),
                cw_spec(0), cw_spec(1), cw_spec(2),
                c0_spec(0), c0_spec(1), c0_spec(2),
                state_spec, pad_spec, pad_spec, pad_spec,
                pl.BlockSpec(memory_space=pl.ANY)]
    cn_spec = pl.BlockSpec((B, 3, LANES), lambda h, c: (0, 0, h))
    out_specs = [pl.BlockSpec((rb, LANES), lambda h, c: (c, h)), state_spec, cn_spec, cn_spec, cn_spec]
    cn_shape = jax.ShapeDtypeStruct((B, 3, H * HEAD_DIM), F32)
    out_shape = [jax.ShapeDtypeStruct(ymix.shape, ymix.dtype),
                 jax.ShapeDtypeStruct((B, H, HEAD_DIM, HEAD_DIM), F32),
                 cn_shape, cn_shape, cn_shape]
    kern = functools.partial(_delta_prompt_kernel, n_heads=H, lc=lc, batch=B)
    return pl.pallas_call(
        kern,
        grid=(H, nblk),
        in_specs=in_specs,
        out_specs=out_specs,
        out_shape=out_shape,
        scratch_shapes=[pltpu.VMEM((B, HEAD_DIM, HEAD_DIM), F32),
                        pltpu.VMEM((3, B, SUBLANES + lc, LANES), F32),
                        pltpu.VMEM((rb, LANES), F32)],
        input_output_aliases={15: 0},
        compiler_params=_cparams(2),
        name="delta_prompt",
    )(qkvz, qkvz, qkvz, qkvz, uba, conv_w, conv_w, conv_w,
      conv0, conv0, conv0, s0, alog_pad, dtb_pad, d_norm, ymix)


def _delta_decode_kernel(q_ref, k_ref, v_ref, z_ref, ba_ref, cw_ref, c0_ref, s0_ref,
                         alog_ref, dtb_ref, dn_ref, *rest, n_heads, bb, has_prev):
    n_any = 2 if has_prev else 1
    y_ref, snew_ref, cnew_ref, qn_scr, kn_scr, vn_scr, o_scr, bg_scr = rest[n_any:]
    H = n_heads
    W = H * HEAD_DIM
    cw = cw_ref[...]

    def conv_silu(part, x_ref):
        x = x_ref[...]
        sl = slice(part * W, (part + 1) * W)
        acc = c0_ref[0, :, sl] * cw[0:1, sl]
        acc = acc + c0_ref[1, :, sl] * cw[1:2, sl]
        acc = acc + c0_ref[2, :, sl] * cw[2:3, sl]
        acc = acc + x * cw[3:4, sl]
        cnew_ref[0, :, sl] = c0_ref[1, :, sl]
        cnew_ref[1, :, sl] = c0_ref[2, :, sl]
        cnew_ref[2, :, sl] = x
        return _silu(acc)

    q = conv_silu(0, q_ref)
    k = conv_silu(1, k_ref)
    v = conv_silu(2, v_ref)
    for h in range(H):
        hs = slice(h * HEAD_DIM, (h + 1) * HEAD_DIM)
        qh, kh = q[:, hs], k[:, hs]
        qn_scr[h] = qh * lax.rsqrt(jnp.sum(qh * qh, axis=-1, keepdims=True) + EPS) * (HEAD_DIM ** -0.5)
        kn_scr[h] = kh * lax.rsqrt(jnp.sum(kh * kh, axis=-1, keepdims=True) + EPS)
        vn_scr[h] = v[:, hs]

    ba = ba_ref[...]
    lane = lax.broadcasted_iota(jnp.int32, ba.shape, 1)
    beta_all = _sigmoid(ba)
    g_all = -jnp.exp(alog_ref[...]) * _softplus(ba + dtb_ref[...])
    bg_scr[...] = jnp.where(lane < H, beta_all, g_all)

    ROWS16 = 2 * SUBLANES

    def split(x):
        hi = _b16(x).astype(F32)
        return hi, _b16(x - hi).astype(F32)

    def stack16(rows):
        pad = jnp.zeros((ROWS16 - len(rows), HEAD_DIM), F32)
        return _b16(jnp.concatenate(list(rows) + [pad], axis=0))

    def per_seq(bi, carry):
        bg = bg_scr[pl.ds(bi, 1), :]
        heads = range(H)
        k_row = [kn_scr[h, pl.ds(bi, 1), :] for h in heads]
        q_row = [qn_scr[h, pl.ds(bi, 1), :] for h in heads]
        v_row = [vn_scr[h, pl.ds(bi, 1), :] for h in heads]
        beta = [bg[:, h:h + 1] for h in heads]
        eg = [jnp.exp(bg[:, H + h:H + h + 1]) for h in heads]
        s = [s0_ref[bi, h] for h in heads]
        k_sp = [split(x) for x in k_row]
        q_sp = [split(x) for x in q_row]
        s_sp = [split(x) for x in s]
        lhs = [stack16([kh, kl, qh, ql]) for (kh, kl), (qh, ql) in zip(k_sp, q_sp)]
        r_hi = [_dot(l_, _b16(sh)) for l_, (sh, _) in zip(lhs, s_sp)]
        r_lo = [_dot(l_, _b16(sl)) for l_, (_, sl) in zip(lhs, s_sp)]
        ks = [a[0:1] + a[1:2] + b[0:1] for a, b in zip(r_hi, r_lo)]
        qs = [a[2:3] + a[3:4] + b[2:3] for a, b in zip(r_hi, r_lo)]
        v_new = [b_ * v_ - (b_ * e_) * ks_ for b_, v_, e_, ks_ in zip(beta, v_row, eg, ks)]
        qk = [jnp.sum(q_ * k_, axis=-1, keepdims=True) for q_, k_ in zip(q_row, k_row)]
        o = [e_ * qs_ + qk_ * vn_ for e_, qs_, qk_, vn_ in zip(eg, qs, qk, v_new)]
        vn_sp = [split(x) for x in v_new]
        outer = [_dot_tn(stack16([kh, kh, kl]), stack16([vh, vl, vh]))
                 for (kh, kl), (vh, vl) in zip(k_sp, vn_sp)]
        for h in heads:
            snew_ref[bi, h] = s[h] * eg[h] + outer[h]
            o_scr[h, pl.ds(bi, 1), :] = o[h]
        return carry

    lax.fori_loop(0, bb, per_seq, 0)

    dn = dn_ref[...]
    z = z_ref[...]
    for h in range(H):
        hs = slice(h * HEAD_DIM, (h + 1) * HEAD_DIM)
        o = o_scr[h]
        o = o * lax.rsqrt(jnp.mean(o * o, axis=-1, keepdims=True) + EPS) * dn
        y_ref[:, hs] = (o * _silu(z[:, hs])).astype(y_ref.dtype)


def delta_decode(qkvz, uba, conv_w, conv0_t, s0_all, layer, alog_pad2, dtb_pad2, d_norm, ymix,
                 snew_all, *, row_off, n_seq, n_heads):
    H = n_heads
    W = H * HEAD_DIM
    bb = SUBLANES
    assert n_seq % bb == 0 and row_off % bb == 0
    ro = row_off // bb
    gate_block = uba.shape[1] // LANES - GATE_TILES

    def col_spec(part):
        return pl.BlockSpec((bb, W), lambda i: (ro + i, part))

    pad_spec = pl.BlockSpec((1, LANES), lambda i: (0, 0))
    state_spec = pl.BlockSpec((None, bb, H, HEAD_DIM, HEAD_DIM), lambda i: (layer, i, 0, 0, 0))
    in_specs = [col_spec(0), col_spec(1), col_spec(2), col_spec(3),
                pl.BlockSpec((bb, LANES), lambda i: (ro + i, gate_block)),
                pl.BlockSpec((4, 3 * W), lambda i: (0, 0)),
                pl.BlockSpec((3, bb, 3 * W), lambda i: (0, i, 0)),
                state_spec,
                pad_spec, pad_spec, pad_spec,
                pl.BlockSpec(memory_space=pl.ANY)]
    args = [qkvz, qkvz, qkvz, qkvz, uba, conv_w, conv0_t, s0_all, alog_pad2, dtb_pad2, d_norm, ymix]
    aliases = {11: 0}
    if snew_all is not None:
        in_specs.append(pl.BlockSpec(memory_space=pl.ANY))
        args.append(snew_all)
        aliases[12] = 1
    out_specs = [pl.BlockSpec((bb, W), lambda i: (ro + i, 0)),
                 state_spec,
                 pl.BlockSpec((3, bb, 3 * W), lambda i: (0, i, 0))]
    out_shape = [jax.ShapeDtypeStruct(ymix.shape, ymix.dtype),
                 jax.ShapeDtypeStruct(s0_all.shape, F32),
                 jax.ShapeDtypeStruct((3, n_seq, 3 * W), F32)]
    kern = functools.partial(_delta_decode_kernel, n_heads=H, bb=bb, has_prev=snew_all is not None)
    head_scr = pltpu.VMEM((H, bb, HEAD_DIM), F32)
    return pl.pallas_call(
        kern,
        grid=(n_seq // bb,),
        in_specs=in_specs,
        out_specs=out_specs,
        out_shape=out_shape,
        scratch_shapes=[head_scr, head_scr, head_scr, head_scr, pltpu.VMEM((bb, LANES), F32)],
        input_output_aliases=aliases,
        compiler_params=_cparams(1),
        name="delta_decode",
    )(*args)


def _s5_prep_kernel(are_ref, aim_ref, ls_ref, bre_ref, bim_ref,
                    lbre_ref, lbim_ref, bbre_ref, bbim_ref, *, n_state, group):
    a_re = are_ref[...]
    a_im = aim_ref[...]
    dt = jnp.exp(ls_ref[...])
    mag = jnp.exp(a_re * dt)
    ang = a_im * dt
    lb_re = mag * jnp.cos(ang)
    lb_im = mag * jnp.sin(ang)
    den = a_re * a_re + a_im * a_im
    num_re = lb_re - 1.0
    num_im = lb_im
    f_re = (num_re * a_re + num_im * a_im) / den
    f_im = (num_im * a_re - num_re * a_im) / den
    lbre_ref[...] = lb_re
    lbim_ref[...] = lb_im
    r = lax.broadcasted_iota(jnp.int32, (n_state, n_state * group), 0)
    cidx = lax.broadcasted_iota(jnp.int32, (n_state, n_state * group), 1)
    expand = jnp.where(_idiv_pow2(cidx, group) == r, 1.0, 0.0).astype(F32)
    fre_x = _dot(f_re, expand, HIGHEST)
    fim_x = _dot(f_im, expand, HIGHEST)
    b_re = bre_ref[...]
    b_im = bim_ref[...]
    bbre_ref[...] = fre_x * b_re - fim_x * b_im
    bbim_ref[...] = fre_x * b_im + fim_x * b_re


def s5_prep(a_re, a_im, log_step, b_re, b_im):
    g, n = a_re.shape
    group = b_re.shape[-1]
    full = lambda shape: pl.BlockSpec(shape, lambda: (0,) * len(shape))
    kern = functools.partial(_s5_prep_kernel, n_state=n, group=group)
    lb_re, lb_im, bb_re, bb_im = pl.pallas_call(
        kern,
        in_specs=[full((g, n)), full((g, n)), full((g, 1)), full((g, n * group)), full((g, n * group))],
        out_specs=[full((g, n)), full((g, n)), full((g, n * group)), full((g, n * group))],
        out_shape=[jax.ShapeDtypeStruct((g, n), F32), jax.ShapeDtypeStruct((g, n), F32),
                   jax.ShapeDtypeStruct((g, n * group), F32), jax.ShapeDtypeStruct((g, n * group), F32)],
        compiler_params=pltpu.CompilerParams(vmem_limit_bytes=VMEM_LIMIT_BYTES),
        name="s5_prep",
    )(a_re, a_im, log_step.reshape(g, 1), b_re.reshape(g, n * group), b_im.reshape(g, n * group))
    return lb_re, lb_im, bb_re.reshape(g, n, group), bb_im.reshape(g, n, group)


def _s5_decode_kernel(u_ref, h0r_ref, h0i_ref, lbr_ref, lbi_ref, wbr_ref, wbi_ref, wc_ref,
                      d_ref, wg_ref, bg_ref, ymix_any,
                      y_ref, hr_out, hi_out,
                      xr_scr, xi_scr, yc_scr, *, n_blocks, blk_in, blk_state):
    del ymix_any
    u = u_ref[...]
    u16 = u.astype(BF16)
    for j in range(n_blocks):
        uj = u16[:, j * blk_in:(j + 1) * blk_in]
        xr_scr[:, j * blk_state:(j + 1) * blk_state] = _dot(uj, wbr_ref[j])
        xi_scr[:, j * blk_state:(j + 1) * blk_state] = _dot(uj, wbi_ref[j])

    n_lanes = n_blocks * blk_state
    lane_chunk = _pick_tile(n_lanes, 1024, LANES)
    for lc in range(n_lanes // lane_chunk):
        ls = slice(lc * lane_chunk, (lc + 1) * lane_chunk)
        lbr, lbi = lbr_ref[:, ls], lbi_ref[:, ls]
        hr, hi = h0r_ref[:, ls], h0i_ref[:, ls]
        xr = lbr * hr - lbi * hi + xr_scr[:, ls]
        xi = lbr * hi + lbi * hr + xi_scr[:, ls]
        xr_scr[:, ls] = xr
        xi_scr[:, ls] = xi
        hr_out[:, ls] = xr
        hi_out[:, ls] = xi

    for j in range(n_blocks):
        ss = slice(j * blk_state, (j + 1) * blk_state)
        yj = _dot(xr_scr[:, ss].astype(BF16), wc_ref[j, :blk_state, :])
        yj = yj + _dot(xi_scr[:, ss].astype(BF16), wc_ref[j, blk_state:, :])
        yc_scr[:, j * blk_in:(j + 1) * blk_in] = yj
    y = _gelu_tanh(yc_scr[...] + d_ref[...] * u)
    gate = _dot(y.astype(BF16), wg_ref[...]) + bg_ref[...]
    y_ref[...] = (y * _sigmoid(gate)).astype(y_ref.dtype)


def s5_mixer_decode(uba, h0_re, h0_im, lb_re, lb_im, wb_re, wb_im, wc, d, w_glu16, b_glu, ymix, *,
                    row_off, n_groups, rb, width, col_block):
    n_blocks, blk_in, blk_state = wb_re.shape
    n_lanes = n_blocks * blk_state
    assert row_off % rb == 0 and rb % (2 * SUBLANES) == 0
    ro = row_off // rb

    def const(shape):
        return pl.BlockSpec(shape, lambda s: (0,) * len(shape))

    state_spec = pl.BlockSpec((rb, n_lanes), lambda s: (s, 0))
    in_specs = [pl.BlockSpec((rb, width), lambda s: (ro + s, 0)),
                state_spec, state_spec,
                const((1, n_lanes)), const((1, n_lanes)),
                const(wb_re.shape), const(wb_im.shape), const(wc.shape),
                const((1, width)), const(w_glu16.shape), const((1, width)),
                pl.BlockSpec(memory_space=pl.ANY)]
    out_specs = [pl.BlockSpec((rb, width), lambda s: (ro + s, col_block)), state_spec, state_spec]
    out_shape = [jax.ShapeDtypeStruct(ymix.shape, ymix.dtype),
                 jax.ShapeDtypeStruct(h0_re.shape, F32),
                 jax.ShapeDtypeStruct(h0_im.shape, F32)]
    kern = functools.partial(_s5_decode_kernel, n_blocks=n_blocks, blk_in=blk_in, blk_state=blk_state)
    return pl.pallas_call(
        kern,
        grid=(n_groups,),
        in_specs=in_specs,
        out_specs=out_specs,
        out_shape=out_shape,
        scratch_shapes=[pltpu.VMEM((rb, n_lanes), F32), pltpu.VMEM((rb, n_lanes), F32),
                        pltpu.VMEM((rb, width), F32)],
        input_output_aliases={11: 0},
        compiler_params=_cparams(1),
        name="s5_mixer_decode",
    )(uba, h0_re, h0_im, lb_re.reshape(1, n_lanes), lb_im.reshape(1, n_lanes),
      wb_re, wb_im, wc, d.reshape(1, width), w_glu16, b_glu.reshape(1, width), ymix)


def _s5_prompt_kernel(u_ref, h0r_ref, h0i_ref, lbr_ref, lbi_ref, wbr_ref, wbi_ref, wc_ref,
                      d_ref, wg_ref, bg_ref, ymix_any,
                      y_ref, hr_out, hi_out,
                      xr_scr, xi_scr, hr_scr, hi_scr, yc_scr, *, rb, n_blocks, blk_in, blk_state):
    del ymix_any
    t_blk = pl.program_id(0)
    nt = pl.num_programs(0)
    R = SUBLANES
    half = R // 2
    ng = rb // R
    nb2 = n_blocks // 2
    n_fold = nb2 * blk_state

    @pl.when(t_blk == 0)
    def _init():
        hr_scr[...] = h0r_ref[...]
        hi_scr[...] = h0i_ref[...]

    u = u_ref[...]
    low_in = lax.broadcasted_iota(jnp.int32, (ng, R, blk_in), 1) < half
    for j in range(nb2):
        ua = u[:, j * blk_in:(j + 1) * blk_in].reshape(ng, R, blk_in)
        ub = u[:, (j + nb2) * blk_in:(j + nb2 + 1) * blk_in].reshape(ng, R, blk_in)
        ra = pltpu.roll(ua, half, 1)
        rbk = pltpu.roll(ub, half, 1)
        a_even, a_odd = jnp.where(low_in, ua, 0.0), jnp.where(low_in, ra, 0.0)
        b_even, b_odd = jnp.where(low_in, 0.0, rbk), jnp.where(low_in, 0.0, ub)
        lhs_a = jnp.stack([a_even, a_odd], axis=1).reshape(2 * ng, R, blk_in)
        lhs_b = jnp.stack([b_even, b_odd], axis=1).reshape(2 * ng, R, blk_in)
        lhs = _b16(jnp.concatenate([lhs_a, lhs_b], axis=-1).reshape(2 * rb, 2 * blk_in))
        ss = slice(j * blk_state, (j + 1) * blk_state)
        xr_scr[:, :, ss] = _dot(lhs, wbr_ref[j]).reshape(2 * ng, R, blk_state)
        xi_scr[:, :, ss] = _dot(lhs, wbi_ref[j]).reshape(2 * ng, R, blk_state)

    lane_chunk = _pick_tile(n_fold, 1024, LANES)
    for lc in range(n_fold // lane_chunk):
        ls = slice(lc * lane_chunk, (lc + 1) * lane_chunk)
        low = lax.broadcasted_iota(jnp.int32, (R, lane_chunk), 0) < half
        l0 = slice(lc * lane_chunk, (lc + 1) * lane_chunk)
        l1 = slice(n_fold + lc * lane_chunk, n_fold + (lc + 1) * lane_chunk)
        lbr = jnp.where(low, lbr_ref[:, l0], lbr_ref[:, l1])
        lbi = jnp.where(low, lbi_ref[:, l0], lbi_ref[:, l1])

        def step(t, carry):
            hr, hi = carry
            xr = lbr * hr - lbi * hi + xr_scr[t, :, ls]
            xi = lbr * hi + lbi * hr + xi_scr[t, :, ls]
            xr_scr[t, :, ls] = xr
            xi_scr[t, :, ls] = xi
            return xr, xi

        hr, hi = lax.fori_loop(0, 2 * ng, step, (hr_scr[:, ls], hi_scr[:, ls]))
        hr_scr[:, ls] = hr
        hi_scr[:, ls] = hi

    @pl.when(t_blk == nt - 1)
    def _fin():
        hr_out[...] = hr_scr[...]
        hi_out[...] = hi_scr[...]

    low_st = lax.broadcasted_iota(jnp.int32, (ng, R, blk_state), 1) < half

    def unfold(x):
        x4 = x.reshape(ng, 2, R, blk_state)
        even, odd = x4[:, 0], x4[:, 1]
        f0 = jnp.where(low_st, even, pltpu.roll(odd, half, 1)).reshape(rb, blk_state)
        f1 = jnp.where(low_st, pltpu.roll(even, half, 1), odd).reshape(rb, blk_state)
        return _b16(f0), _b16(f1)

    for j in range(nb2):
        ss = slice(j * blk_state, (j + 1) * blk_state)
        r0, r1 = unfold(xr_scr[:, :, ss])
        i0, i1 = unfold(xi_scr[:, :, ss])
        j1 = j + nb2
        yc_scr[:, j * blk_in:(j + 1) * blk_in] = (
            _dot(r0, wc_ref[j, :blk_state, :]) + _dot(i0, wc_ref[j, blk_state:, :]))
        yc_scr[:, j1 * blk_in:(j1 + 1) * blk_in] = (
            _dot(r1, wc_ref[j1, :blk_state, :]) + _dot(i1, wc_ref[j1, blk_state:, :]))
    y = _gelu_tanh(yc_scr[...] + d_ref[...] * u)
    gate = _dot(y.astype(BF16), wg_ref[...]) + bg_ref[...]
    y_ref[...] = (y * _sigmoid(gate)).astype(y_ref.dtype)


def s5_mixer_prompt(uba, h0_re, h0_im, lb_re, lb_im, wb2_re, wb2_im, wc, d, w_glu16, b_glu, ymix, *,
                    n_time_blocks, rb, width, col_block):
    nb2, blk_in2, blk_state = wb2_re.shape
    blk_in = blk_in2 // 2
    n_blocks = 2 * nb2
    n_lanes = n_blocks * blk_state
    n_fold = n_lanes // 2
    assert rb % (2 * SUBLANES) == 0

    def const(shape):
        return pl.BlockSpec(shape, lambda t: (0,) * len(shape))

    state_spec = const((SUBLANES, n_fold))
    in_specs = [pl.BlockSpec((rb, width), lambda t: (t, 0)),
                state_spec, state_spec,
                const((1, n_lanes)), const((1, n_lanes)),
                const(wb2_re.shape), const(wb2_im.shape), const(wc.shape),
                const((1, width)), const(w_glu16.shape), const((1, width)),
                pl.BlockSpec(memory_space=pl.ANY)]
    out_specs = [pl.BlockSpec((rb, width), lambda t: (t, col_block)), state_spec, state_spec]
    out_shape = [jax.ShapeDtypeStruct(ymix.shape, ymix.dtype),
                 jax.ShapeDtypeStruct(h0_re.shape, F32),
                 jax.ShapeDtypeStruct(h0_im.shape, F32)]
    kern = functools.partial(_s5_prompt_kernel, rb=rb, n_blocks=n_blocks, blk_in=blk_in,
                             blk_state=blk_state)
    scan_scr = pltpu.VMEM((2 * rb // SUBLANES, SUBLANES, n_fold), F32)
    return pl.pallas_call(
        kern,
        grid=(n_time_blocks,),
        in_specs=in_specs,
        out_specs=out_specs,
        out_shape=out_shape,
        scratch_shapes=[scan_scr, scan_scr,
                        pltpu.VMEM((SUBLANES, n_fold), F32), pltpu.VMEM((SUBLANES, n_fold), F32),
                        pltpu.VMEM((rb, width), F32)],
        input_output_aliases={11: 0},
        compiler_params=_cparams(1),
        name="s5_mixer_prompt",
    )(uba, h0_re, h0_im, lb_re.reshape(1, n_lanes), lb_im.reshape(1, n_lanes),
      wb2_re, wb2_im, wc, d.reshape(1, width), w_glu16, b_glu.reshape(1, width), ymix)


def _s5_block_weights(bb_re, bb_im, c_re, c_im):
    g, n, grp = bb_re.shape
    r = S5_GROUPS_PER_BLOCK
    nb = g // r
    eye = jnp.eye(r, dtype=F32)

    def in_blocks(bb):
        w = jnp.einsum("jsnc,rs->jrcsn", bb.reshape(nb, r, n, grp), eye)
        return w.reshape(nb, r * grp, r * n).astype(BF16)

    def out_blocks(cc):
        w = jnp.einsum("jrcn,rs->jsnrc", cc.reshape(nb, r, grp, n), eye)
        return w.reshape(nb, r * n, r * grp)

    wc = jnp.concatenate([out_blocks(c_re), -out_blocks(c_im)], axis=1).astype(BF16)
    return in_blocks(bb_re), in_blocks(bb_im), wc


def kernel(x_prompt, x_sample, state_delta, state_conv, state_s5_re, state_s5_im, norm_mix, w_in, conv_w, delta_a_log, delta_dt_bias, delta_norm, s5_a_re, s5_a_im, s5_log_step, s5_b_re, s5_b_im, s5_c_re, s5_c_im, s5_d, s5_w_glu, s5_b_glu, w_out, norm_ffn, w_ffn_in, w_ffn_out, norm_final):
    B, L, D = x_prompt.shape
    Bs = x_sample.shape[0]
    depth = w_in.shape[0]
    H = delta_a_log.shape[1]
    Wd = H * HEAD_DIM
    G, N = s5_a_re.shape[1:]
    grp = s5_b_re.shape[-1]
    Ws = G * grp
    d_ff = w_ffn_out.shape[1]
    Mp = B * L
    M = Mp + Bs
    half = SUBLANES // 2
    assert B == half and x_sample.shape[1] == 1 and Wd % Ws == 0
    uba_w = Ws + GATE_TILES * LANES

    x = jnp.concatenate([jnp.transpose(x_prompt, (1, 0, 2)).reshape(Mp, D),
                         x_sample.reshape(Bs, D)], axis=0)

    gate_cols = 4 * Wd
    w_qkvz16 = w_in[:, :, :gate_cols].astype(BF16)
    w_uba = jnp.concatenate([w_in[:, :, gate_cols + 2 * H:], w_in[:, :, gate_cols:gate_cols + 2 * H],
                             jnp.zeros((depth, D, GATE_TILES * LANES - 2 * H), F32)], axis=2).astype(BF16)
    pad_h = lambda a: jnp.pad(a, (0, LANES - H)).reshape(1, LANES)
    pad_h2 = lambda a: jnp.pad(a, (H, LANES - 2 * H)).reshape(1, LANES)
    zero_s = jnp.zeros((B, H, HEAD_DIM, HEAD_DIM), F32)
    zero_c = jnp.zeros((B, 3, 3 * Wd), F32)
    zero_h = jnp.zeros((SUBLANES, G * N // 2), F32)
    s5_rb = _pick_tile(Mp, 128, 2 * SUBLANES)
    s5_rb_dec = _pick_tile(Bs, 32, 2 * SUBLANES)

    w_ffn_out16 = cast_bf16(w_ffn_out)
    s_dec = None
    outs = {k: [] for k in ("dp", "cp", "rp", "ip", "cs", "rs", "is")}
    for l in range(depth):
        hmix = rmsnorm_rows(x, norm_mix[l], BF16)
        qkvz = matmul(hmix, w_qkvz16, l, n=gate_cols, tn=1024)
        uba = matmul(hmix, w_uba, l, n=uba_w, tn=1280)

        ymix = jnp.zeros((M, Wd + Ws), BF16)
        dnorm = delta_norm[l].reshape(1, HEAD_DIM)
        ymix, s_p, cnq, cnk, cnv = delta_prompt(
            qkvz, uba, conv_w[l], zero_c, zero_s, pad_h(delta_a_log[l]), pad_h(delta_dt_bias[l]),
            dnorm, ymix, seq=L, batch=B, n_heads=H)
        ymix, s_dec, cn_s = delta_decode(
            qkvz, uba, conv_w[l], jnp.transpose(state_conv[l], (1, 0, 2)), state_delta, l,
            pad_h2(delta_a_log[l]), pad_h2(delta_dt_bias[l]), dnorm, ymix, s_dec,
            row_off=Mp, n_seq=Bs, n_heads=H)
        outs["dp"].append(s_p)
        outs["cp"].append(jnp.concatenate([cnq, cnk, cnv], axis=-1))
        outs["cs"].append(jnp.transpose(cn_s, (1, 0, 2)))

        lb_re, lb_im, bb_re, bb_im = s5_prep(s5_a_re[l], s5_a_im[l], s5_log_step[l],
                                             s5_b_re[l], s5_b_im[l])
        wb_re, wb_im, wc = _s5_block_weights(bb_re, bb_im, s5_c_re[l], s5_c_im[l])
        w_glu16 = s5_w_glu[l].astype(BF16)
        nb2 = wb_re.shape[0] // 2
        ymix, hr_p, hi_p = s5_mixer_prompt(
            uba, zero_h, zero_h, lb_re, lb_im,
            jnp.concatenate([wb_re[:nb2], wb_re[nb2:]], axis=1),
            jnp.concatenate([wb_im[:nb2], wb_im[nb2:]], axis=1),
            wc, s5_d[l], w_glu16, s5_b_glu[l], ymix,
            n_time_blocks=Mp // s5_rb, rb=s5_rb, width=Ws, col_block=Wd // Ws)
        ymix, hr_s, hi_s = s5_mixer_decode(
            uba, state_s5_re[l].reshape(Bs, G * N), state_s5_im[l].reshape(Bs, G * N),
            lb_re, lb_im, wb_re, wb_im, wc, s5_d[l], w_glu16, s5_b_glu[l], ymix,
            row_off=Mp, n_groups=Bs // s5_rb_dec, rb=s5_rb_dec, width=Ws, col_block=Wd // Ws)
        unfold = lambda a: jnp.transpose(a.reshape(2, B, G * N // 2), (1, 0, 2)).reshape(B, G, N)
        outs["rp"].append(unfold(hr_p))
        outs["ip"].append(unfold(hi_p))
        outs["rs"].append(hr_s.reshape(Bs, G, N))
        outs["is"].append(hi_s.reshape(Bs, G, N))

        x = matmul(ymix, w_out, l, n=D, res=x)

        hffn = rmsnorm_rows(x, norm_ffn[l], BF16)
        hidden = matmul(hffn, w_ffn_in, l, n=d_ff, col_off2=d_ff, out_dtype=BF16, tm=2080, tn=256,
                        rows_outer=True, x_single_buffer=True)
        x = matmul(hidden, w_ffn_out16, l, n=D, res=x, tm=520, tn=512, rows_outer=True)

    y = rmsnorm_rows(x, norm_final, F32)
    y_prompt = jnp.transpose(y[:Mp].reshape(L, B, D), (1, 0, 2))
    y_sample = y[Mp:].reshape(Bs, 1, D)
    st = lambda k: jnp.stack(outs[k])
    return (y_prompt, y_sample, st("dp"), st("cp"), st("rp"), st("ip"),
            s_dec, st("cs"), st("rs"), st("is"))
```

```python
import functools

import jax
import jax.numpy as jnp
from jax import lax
from jax.experimental import pallas as pl
from jax.experimental.pallas import tpu as pltpu

F32 = jnp.float32
BF16 = jnp.bfloat16
EPS = 1e-6

LANES = 128
SUBLANES = 8
VMEM_LIMIT_BYTES = 56 * 1024 * 1024
HEAD_DIM = 128
DELTA_CHUNK = 128
SOLVE_BLOCK = 16
S5_GROUPS_PER_BLOCK = 8
GATE_TILES = 4
HIGHEST = lax.Precision.HIGHEST


def _cparams(n_axes):
    return pltpu.CompilerParams(dimension_semantics=("arbitrary",) * n_axes,
                                vmem_limit_bytes=VMEM_LIMIT_BYTES)


def _pick_tile(n, target, mult):
    best = None
    for d in range(mult, min(n, target) + 1, mult):
        if n % d == 0:
            best = d
    return n if best is None else best


def _idiv_pow2(x, c):
    shift = c.bit_length() - 1
    assert c == 1 << shift
    return lax.shift_right_logical(x, jnp.int32(shift))


def _sigmoid(x):
    return 1.0 / (1.0 + jnp.exp(-x))


def _silu(x):
    return x * _sigmoid(x)


def _softplus(x):
    return jnp.maximum(x, 0.0) + jnp.log(1.0 + jnp.exp(-jnp.abs(x)))


def _gelu_tanh(x):
    c = 0.7978845608028654
    return 0.5 * x * (1.0 + jnp.tanh(c * (x + 0.044715 * (x * x * x))))


def _dot(a, b, precision=None):
    return jnp.dot(a, b, preferred_element_type=F32, precision=precision)


def _dot_nt(a, b, precision=None):
    return lax.dot_general(a, b, (((1,), (1,)), ((), ())), preferred_element_type=F32,
                           precision=precision)


def _dot_tn(a, b, precision=None):
    return lax.dot_general(a, b, (((0,), (0,)), ((), ())), preferred_element_type=F32,
                           precision=precision)


def _b16(x):
    return x.astype(BF16)


def _rmsnorm_kernel(x_ref, w_ref, o_ref):
    x = x_ref[...]
    y = x * lax.rsqrt(jnp.mean(x * x, axis=-1, keepdims=True) + EPS)
    o_ref[...] = (y * w_ref[...]).astype(o_ref.dtype)


def rmsnorm_rows(x, w, out_dtype):
    m, d = x.shape
    tr = _pick_tile(m, 512, 16)
    return pl.pallas_call(
        _rmsnorm_kernel,
        grid=(m // tr,),
        in_specs=[pl.BlockSpec((tr, d), lambda i: (i, 0)),
                  pl.BlockSpec((1, d), lambda i: (0, 0))],
        out_specs=pl.BlockSpec((tr, d), lambda i: (i, 0)),
        out_shape=jax.ShapeDtypeStruct((m, d), out_dtype),
        compiler_params=_cparams(1),
        name="rmsnorm_rows",
    )(x, w.reshape(1, d))


def _mm_kernel(*refs, n_w, has_res, has_norm, has_wscale, emit_stats, k_total):
    it = iter(refs)
    x_ref = next(it)
    w_refs = [next(it) for _ in range(n_w)]
    wscale_ref = next(it) if has_wscale else None
    ssq_ref = next(it) if has_norm else None
    res_ref = next(it) if has_res else None
    o_ref = next(it)
    x = x_ref[...]
    parts = []
    for w_ref in w_refs:
        w = w_ref[...]
        if has_wscale:
            g = wscale_ref[...]
            w = w * jnp.concatenate([g] * (w.shape[1] // LANES), axis=1)
        parts.append(_dot(x, _b16(w)))
    if has_norm:
        r = lax.rsqrt(ssq_ref[:, 0:1] * (1.0 / k_total) + EPS)
        parts = [p * r for p in parts]
    y = parts[0]
    if n_w == 2:
        y = _silu(y) * parts[1]
    if has_res:
        y = y + res_ref[...]
    o_ref[...] = y.astype(o_ref.dtype)
    if emit_stats:
        o16_ref = next(it)
        ssq_out_ref = next(it)
        o16_ref[...] = _b16(y)
        part = jnp.broadcast_to(jnp.sum(y * y, axis=-1, keepdims=True), ssq_out_ref.shape)
        j = pl.program_id(1)

        @pl.when(j == 0)
        def _first():
            ssq_out_ref[...] = part

        @pl.when(j > 0)
        def _rest():
            ssq_out_ref[...] += part


def matmul(x, w, layer, *, n, col_off=0, col_off2=None, res=None, out_dtype=F32,
           tm=1040, tn=512, rows_outer=False, x_single_buffer=False,
           row_ssq=None, w_scale=None, emit_stats=False):
    m, k_total = x.shape
    assert w.shape[1] == k_total
    tm = _pick_tile(m, tm, 16)
    tn = _pick_tile(n, tn, LANES)
    assert col_off % tn == 0 and (col_off2 is None or col_off2 % tn == 0)
    assert rows_outer or not emit_stats
    n_w = 1 if col_off2 is None else 2
    if rows_outer:
        grid = (m // tm, n // tn)
        ij = lambda a, b: (a, b)
    else:
        grid = (n // tn, m // tm)
        ij = lambda a, b: (b, a)
    x_mode = dict(pipeline_mode=pl.Buffered(1)) if (x_single_buffer and rows_outer) else {}
    in_specs = [pl.BlockSpec((tm, k_total), lambda a, b: (ij(a, b)[0], 0), **x_mode)]
    args = [x]
    for off in (col_off, col_off2)[:n_w]:
        in_specs.append(pl.BlockSpec((None, k_total, tn),
                                     lambda a, b, o=off // tn: (layer, 0, ij(a, b)[1] + o)))
        args.append(w)
    if w_scale is not None:
        in_specs.append(pl.BlockSpec((None, k_total, LANES), lambda a, b: (layer, 0, 0)))
        args.append(w_scale)
    if row_ssq is not None:
        in_specs.append(pl.BlockSpec((tm, LANES), lambda a, b: (ij(a, b)[0], 0)))
        args.append(row_ssq)
    if res is not None:
        in_specs.append(pl.BlockSpec((tm, tn), lambda a, b: ij(a, b)))
        args.append(res)
    out_specs = [pl.BlockSpec((tm, tn), lambda a, b: ij(a, b))]
    out_shape = [jax.ShapeDtypeStruct((m, n), out_dtype)]
    if emit_stats:
        out_specs += [pl.BlockSpec((tm, tn), lambda a, b: ij(a, b)),
                      pl.BlockSpec((tm, LANES), lambda a, b: (ij(a, b)[0], 0))]
        out_shape += [jax.ShapeDtypeStruct((m, n), BF16), jax.ShapeDtypeStruct((m, LANES), F32)]
    kern = functools.partial(_mm_kernel, n_w=n_w, has_res=res is not None,
                             has_norm=row_ssq is not None, has_wscale=w_scale is not None,
                             emit_stats=emit_stats, k_total=k_total)
    outs = pl.pallas_call(
        kern,
        grid=grid,
        in_specs=in_specs,
        out_specs=out_specs,
        out_shape=out_shape,
        compiler_params=_cparams(2),
        name="matmul_rows",
    )(*args)
    return outs if emit_stats else outs[0]


def _row_stats_kernel(x_ref, o16_ref, ssq_ref):
    x = x_ref[...]
    o16_ref[...] = _b16(x)
    ssq_ref[...] = jnp.broadcast_to(jnp.sum(x * x, axis=-1, keepdims=True), ssq_ref.shape)


def row_stats(x):
    m, d = x.shape
    tr = _pick_tile(m, 512, 16)
    return pl.pallas_call(
        _row_stats_kernel,
        grid=(m // tr,),
        in_specs=[pl.BlockSpec((tr, d), lambda i: (i, 0))],
        out_specs=[pl.BlockSpec((tr, d), lambda i: (i, 0)), pl.BlockSpec((tr, LANES), lambda i: (i, 0))],
        out_shape=[jax.ShapeDtypeStruct((m, d), BF16), jax.ShapeDtypeStruct((m, LANES), F32)],
        compiler_params=_cparams(1),
        name="row_stats",
    )(x)


def _cast_kernel(x_ref, o_ref):
    o_ref[...] = x_ref[...].astype(o_ref.dtype)


def cast_bf16(w):
    depth, k, n = w.shape
    tr = _pick_tile(k, 512, 16)
    return pl.pallas_call(
        _cast_kernel,
        grid=(depth, k // tr),
        in_specs=[pl.BlockSpec((None, tr, n), lambda l, i: (l, i, 0))],
        out_specs=pl.BlockSpec((None, tr, n), lambda l, i: (l, i, 0)),
        out_shape=jax.ShapeDtypeStruct(w.shape, BF16),
        compiler_params=_cparams(2),
        name="cast_bf16",
    )(w)


def _lane_pick(x, idx):
    lane = lax.broadcasted_iota(jnp.int32, x.shape, 1)
    return jnp.sum(jnp.where(lane == idx, x, 0.0), axis=-1, keepdims=True)


def _unit_lower_inverse_minus_eye(a_list):
    C = DELTA_CHUNK
    assert C == 8 * SOLVE_BLOCK
    row = lax.broadcasted_iota(jnp.int32, (C, C), 0)
    col = lax.broadcasted_iota(jnp.int32, (C, C), 1)
    same_block = _idiv_pow2(row, SOLVE_BLOCK) == _idiv_pow2(col, SOLVE_BLOCK)
    d = [jnp.where(same_block, a, 0.0) for a in a_list]
    e = [a - dd for a, dd in zip(a_list, d)]
    p = [-dd for dd in d]
    td = list(p)
    for _ in range(3):
        p = [_dot(_b16(x), _b16(x)) for x in p]
        td = [t + x + _dot(_b16(t), _b16(x)) for t, x in zip(td, p)]
    f = [ee + _dot(_b16(t), _b16(ee)) for t, ee in zip(td, e)]
    f2 = [_dot(_b16(x), _b16(x)) for x in f]
    f4 = [_dot(_b16(x), _b16(x)) for x in f2]
    h1 = [x2 - x - _dot(_b16(x), _b16(x2)) for x, x2 in zip(f, f2)]
    gd = [h + x4 + _dot(_b16(h), _b16(x4)) for h, x4 in zip(h1, f4)]
    return [g + t + _dot(_b16(g), _b16(t)) for g, t in zip(gd, td)]


def _delta_prompt_kernel(q_ref, k_ref, v_ref, z_ref, ba_ref, cwq_ref, cwk_ref, cwv_ref,
                         c0q_ref, c0k_ref, c0v_ref, s0_ref, alog_ref, dtb_ref, dn_ref, ymix_any,
                         y_ref, sfin_ref, cnq_ref, cnk_ref, cnv_ref,
                         s_scr, xbuf, ybuf, *, n_heads, lc, batch):
    del ymix_any
    h = pl.program_id(0)
    c = pl.program_id(1)
    nc = pl.num_programs(1)
    C = DELTA_CHUNK
    B = batch
    n_chunks = lc // C
    PAD = SUBLANES

    def rows_of(ref, b):
        return ref[pl.ds(b, lc, stride=B), :]

    @pl.when(c == 0)
    def _init():
        s_scr[...] = s0_ref[...]
        for b in range(B):
            xbuf[0, b, PAD - 3:PAD, :] = c0q_ref[b]
            xbuf[1, b, PAD - 3:PAD, :] = c0k_ref[b]
            xbuf[2, b, PAD - 3:PAD, :] = c0v_ref[b]

    def conv_silu(j, b, x_ref, cw_ref, cn_ref):
        xbuf[j, b, PAD:PAD + lc, :] = rows_of(x_ref, b)
        cw = cw_ref[...]
        acc = xbuf[j, b, PAD - 3:PAD - 3 + lc, :] * cw[0:1, :]
        acc = acc + xbuf[j, b, PAD - 2:PAD - 2 + lc, :] * cw[1:2, :]
        acc = acc + xbuf[j, b, PAD - 1:PAD - 1 + lc, :] * cw[2:3, :]
        acc = acc + xbuf[j, b, PAD:PAD + lc, :] * cw[3:4, :]
        tail = xbuf[j, b, PAD + lc - 3:PAD + lc, :]
        xbuf[j, b, PAD - 3:PAD, :] = tail

        @pl.when(c == nc - 1)
        def _():
            cn_ref[b] = tail

        return _silu(acc)

    a_log = _lane_pick(alog_ref[...], h)
    dt_b = _lane_pick(dtb_ref[...], h)
    neg_a = -jnp.exp(a_log)

    crow = lax.broadcasted_iota(jnp.int32, (C, C), 0)
    ccol = lax.broadcasted_iota(jnp.int32, (C, C), 1)
    lane = lax.broadcasted_iota(jnp.int32, (C, LANES), 1)

    qs, ks, vs, betas, gmat = [], [], [], [], jnp.zeros((C, LANES), F32)
    for b in range(B):
        q = conv_silu(0, b, q_ref, cwq_ref, cnq_ref)
        k = conv_silu(1, b, k_ref, cwk_ref, cnk_ref)
        v = conv_silu(2, b, v_ref, cwv_ref, cnv_ref)
        q = q * lax.rsqrt(jnp.sum(q * q, axis=-1, keepdims=True) + EPS) * (HEAD_DIM ** -0.5)
        k = k * lax.rsqrt(jnp.sum(k * k, axis=-1, keepdims=True) + EPS)
        ba = rows_of(ba_ref, b)
        beta = _sigmoid(_lane_pick(ba, h))
        g = neg_a * _softplus(_lane_pick(ba, h + n_heads) + dt_b)
        for ci in range(n_chunks):
            sl = slice(ci * C, (ci + 1) * C)
            u = b * n_chunks + ci
            qs.append(q[sl]); ks.append(k[sl]); vs.append(v[sl]); betas.append(beta[sl])
            gmat = gmat + jnp.where(lane == u, g[sl], 0.0)
    n_units = B * n_chunks
    assert n_units <= LANES

    tri = jnp.where(crow >= ccol, 1.0, 0.0).astype(F32)
    gc_mat = _dot(tri, gmat, HIGHEST)
    gc_t = jnp.transpose(gc_mat)

    lower_incl = crow >= ccol
    lower_strict = crow > ccol

    def chunk_local(us):
        pick = lambda xs: [xs[u] for u in us]
        q_, k_, v_, b_ = pick(qs), pick(ks), pick(vs), pick(betas)
        gcs = [gc_mat[:, u:u + 1] for u in us]
        egc = [jnp.exp(x) for x in gcs]
        kb = [x * y for x, y in zip(k_, b_)]
        k16 = [_b16(x) for x in k_]
        kq = [_dot_nt(_b16(jnp.concatenate([x, y], axis=0)), z) for x, y, z in zip(kb, q_, k16)]
        decay_incl = [jnp.exp(jnp.where(lower_incl, g - gc_t[u:u + 1, :], -jnp.inf))
                      for g, u in zip(gcs, us)]
        a_strict = [jnp.where(lower_strict, x[:C] * d_, 0.0) for x, d_ in zip(kq, decay_incl)]
        attn16 = [_b16(x[C:] * d_) for x, d_ in zip(kq, decay_incl)]
        md = _unit_lower_inverse_minus_eye(a_strict)
        rhs = [jnp.concatenate([x * y, z * e], axis=-1)
               for x, y, z, e in zip(v_, b_, kb, egc)]
        sol = [r + _dot(_b16(m), _b16(r)) for m, r in zip(md, rhs)]
        gc_last = [x[C - 1:C, :] for x in gcs]
        kcum_qdec16 = [_b16(jnp.concatenate([s_[:, HEAD_DIM:], x * e], axis=0))
                       for s_, x, e in zip(sol, q_, egc)]
        kdec16 = [_b16(x * jnp.exp(gl - g)) for x, gl, g in zip(k_, gc_last, gcs)]
        g_last = [jnp.exp(x) for x in gc_last]
        value = [s_[:, :HEAD_DIM] for s_ in sol]
        return attn16, value, kcum_qdec16, kdec16, g_last

    attn16, value, kcum_qdec16, kdec16, g_last = chunk_local(list(range(n_units)))
    s = [s_scr[b] for b in range(B)]
    o_units = [None] * n_units
    for ci in range(n_chunks):
        us = [b * n_chunks + ci for b in range(B)]
        ks_qs = [_dot(kcum_qdec16[u], _b16(s[b])) for b, u in enumerate(us)]
        v_new16 = [_b16(value[u] - x[:C]) for u, x in zip(us, ks_qs)]
        for b, u in enumerate(us):
            o_units[u] = ks_qs[b][C:] + _dot(attn16[u], v_new16[b])
            s[b] = s[b] * g_last[u] + _dot_tn(kdec16[u], v_new16[b])
    for b in range(B):
        s_scr[b] = s[b]

    @pl.when(c == nc - 1)
    def _():
        for b in range(B):
            sfin_ref[b] = s[b]

    dn = dn_ref[...]
    for b in range(B):
        o = jnp.concatenate(o_units[b * n_chunks:(b + 1) * n_chunks], axis=0)
        o = o * lax.rsqrt(jnp.mean(o * o, axis=-1, keepdims=True) + EPS) * dn
        ybuf[pl.ds(b, lc, stride=B), :] = o * _silu(rows_of(z_ref, b))
    y_ref[...] = ybuf[...].astype(y_ref.dtype)


def delta_prompt(qkvz, uba, conv_w, conv0, s0, alog_pad, dtb_pad, d_norm, ymix, *,
                 seq, batch, n_heads):
    H = n_heads
    B = batch
    lc = _pick_tile(seq, 512, DELTA_CHUNK)
    assert lc % DELTA_CHUNK == 0
    nblk = seq // lc
    gate_block = uba.shape[1] // LANES - GATE_TILES
    rb = lc * B

    def col_spec(part):
        return pl.BlockSpec((rb, LANES), lambda h, c: (c, part * H + h))

    def cw_spec(part):
        return pl.BlockSpec((4, LANES), lambda h, c: (0, part * H + h))

    def c0_spec(part):
        return pl.BlockSpec((B, 3, LANES), lambda h, c: (0, 0, part * H + h))

    pad_spec = pl.BlockSpec((1, LANES), lambda h, c: (0, 0))
    state_spec = pl.BlockSpec((B, None, HEAD_DIM, HEAD_DIM), lambda h, c: (0, h, 0, 0))
    in_specs = [col_spec(0), col_spec(1), col_spec(2), col_spec(3),
                pl.BlockSpec((rb, LANES), lambda h, c: (c, gate_block)),
                cw_spec(0), cw_spec(1), cw_spec(2),
                c0_spec(0), c0_spec(1), c0_spec(2),
                state_spec, pad_spec, pad_spec, pad_spec,
                pl.BlockSpec(memory_space=pl.ANY)]
    cn_spec = pl.BlockSpec((B, 3, LANES), lambda h, c: (0, 0, h))
    out_specs = [pl.BlockSpec((rb, LANES), lambda h, c: (c, h)), state_spec, cn_spec, cn_spec, cn_spec]
    cn_shape = jax.ShapeDtypeStruct((B, 3, H * HEAD_DIM), F32)
    out_shape = [jax.ShapeDtypeStruct(ymix.shape, ymix.dtype),
                 jax.ShapeDtypeStruct((B, H, HEAD_DIM, HEAD_DIM), F32),
                 cn_shape, cn_shape, cn_shape]
    kern = functools.partial(_delta_prompt_kernel, n_heads=H, lc=lc, batch=B)
    return pl.pallas_call(
        kern,
        grid=(H, nblk),
        in_specs=in_specs,
        out_specs=out_specs,
        out_shape=out_shape,
        scratch_shapes=[pltpu.VMEM((B, HEAD_DIM, HEAD_DIM), F32),
                        pltpu.VMEM((3, B, SUBLANES + lc, LANES), F32),
                        pltpu.VMEM((rb, LANES), F32)],
        input_output_aliases={15: 0},
        compiler_params=_cparams(2),
        name="delta_prompt",
    )(qkvz, qkvz, qkvz, qkvz, uba, conv_w, conv_w, conv_w,
      conv0, conv0, conv0, s0, alog_pad, dtb_pad, d_norm, ymix)


def _delta_decode_kernel(q_ref, k_ref, v_ref, z_ref, ba_ref, cw_ref, c0_ref, s0_ref,
                         alog_ref, dtb_ref, dn_ref, *rest, n_heads, bb, has_prev):
    n_any = 2 if has_prev else 1
    y_ref, snew_ref, cnew_ref, qn_scr, kn_scr, vn_scr, o_scr, bg_scr = rest[n_any:]
    H = n_heads
    W = H * HEAD_DIM
    cw = cw_ref[...]

    def conv_silu(part, x_ref):
        x = x_ref[...]
        sl = slice(part * W, (part + 1) * W)
        acc = c0_ref[0, :, sl] * cw[0:1, sl]
        acc = acc + c0_ref[1, :, sl] * cw[1:2, sl]
        acc = acc + c0_ref[2, :, sl] * cw[2:3, sl]
        acc = acc + x * cw[3:4, sl]
        cnew_ref[0, :, sl] = c0_ref[1, :, sl]
        cnew_ref[1, :, sl] = c0_ref[2, :, sl]
        cnew_ref[2, :, sl] = x
        return _silu(acc)

    q = conv_silu(0, q_ref)
    k = conv_silu(1, k_ref)
    v = conv_silu(2, v_ref)
    for h in range(H):
        hs = slice(h * HEAD_DIM, (h + 1) * HEAD_DIM)
        qh, kh = q[:, hs], k[:, hs]
        qn_scr[h] = qh * lax.rsqrt(jnp.sum(qh * qh, axis=-1, keepdims=True) + EPS) * (HEAD_DIM ** -0.5)
        kn_scr[h] = kh * lax.rsqrt(jnp.sum(kh * kh, axis=-1, keepdims=True) + EPS)
        vn_scr[h] = v[:, hs]

    ba = ba_ref[...]
    lane = lax.broadcasted_iota(jnp.int32, ba.shape, 1)
    beta_all = _sigmoid(ba)
    g_all = -jnp.exp(alog_ref[...]) * _softplus(ba + dtb_ref[...])
    bg_scr[...] = jnp.where(lane < H, beta_all, g_all)

    ROWS16 = 2 * SUBLANES

    def split(x):
        hi = _b16(x).astype(F32)
        return hi, _b16(x - hi).astype(F32)

    def stack16(rows):
        pad = jnp.zeros((ROWS16 - len(rows), HEAD_DIM), F32)
        return _b16(jnp.concatenate(list(rows) + [pad], axis=0))

    def per_seq(bi, carry):
        bg = bg_scr[pl.ds(bi, 1), :]
        heads = range(H)
        k_row = [kn_scr[h, pl.ds(bi, 1), :] for h in heads]
        q_row = [qn_scr[h, pl.ds(bi, 1), :] for h in heads]
        v_row = [vn_scr[h, pl.ds(bi, 1), :] for h in heads]
        beta = [bg[:, h:h + 1] for h in heads]
        eg = [jnp.exp(bg[:, H + h:H + h + 1]) for h in heads]
        s = [s0_ref[bi, h] for h in heads]
        k_sp = [split(x) for x in k_row]
        q_sp = [split(x) for x in q_row]
        s_sp = [split(x) for x in s]
        lhs = [stack16([kh, kl, qh, ql]) for (kh, kl), (qh, ql) in zip(k_sp, q_sp)]
        r_hi = [_dot(l_, _b16(sh)) for l_, (sh, _) in zip(lhs, s_sp)]
        r_lo = [_dot(l_, _b16(sl)) for l_, (_, sl) in zip(lhs, s_sp)]
        ks = [a[0:1] + a[1:2] + b[0:1] for a, b in zip(r_hi, r_lo)]
        qs = [a[2:3] + a[3:4] + b[2:3] for a, b in zip(r_hi, r_lo)]
        v_new = [b_ * v_ - (b_ * e_) * ks_ for b_, v_, e_, ks_ in zip(beta, v_row, eg, ks)]
        qk = [jnp.sum(q_ * k_, axis=-1, keepdims=True) for q_, k_ in zip(q_row, k_row)]
        o = [e_ * qs_ + qk_ * vn_ for e_, qs_, qk_, vn_ in zip(eg, qs, qk, v_new)]
        vn_sp = [split(x) for x in v_new]
        outer = [_dot_tn(stack16([kh, kh, kl]), stack16([vh, vl, vh]))
                 for (kh, kl), (vh, vl) in zip(k_sp, vn_sp)]
        for h in heads:
            snew_ref[bi, h] = s[h] * eg[h] + outer[h]
            o_scr[h, pl.ds(bi, 1), :] = o[h]
        return carry

    lax.fori_loop(0, bb, per_seq, 0)

    dn = dn_ref[...]
    z = z_ref[...]
    for h in range(H):
        hs = slice(h * HEAD_DIM, (h + 1) * HEAD_DIM)
        o = o_scr[h]
        o = o * lax.rsqrt(jnp.mean(o * o, axis=-1, keepdims=True) + EPS) * dn
        y_ref[:, hs] = (o * _silu(z[:, hs])).astype(y_ref.dtype)


def delta_decode(qkvz, uba, conv_w, conv0_t, s0_all, layer, alog_pad2, dtb_pad2, d_norm, ymix,
                 snew_all, *, row_off, n_seq, n_heads):
    H = n_heads
    W = H * HEAD_DIM
    bb = SUBLANES
    assert n_seq % bb == 0 and row_off % bb == 0
    ro = row_off // bb
    gate_block = uba.shape[1] // LANES - GATE_TILES

    def col_spec(part):
        return pl.BlockSpec((bb, W), lambda i: (ro + i, part))

    pad_spec = pl.BlockSpec((1, LANES), lambda i: (0, 0))
    state_spec = pl.BlockSpec((None, bb, H, HEAD_DIM, HEAD_DIM), lambda i: (layer, i, 0, 0, 0))
    in_specs = [col_spec(0), col_spec(1), col_spec(2), col_spec(3),
                pl.BlockSpec((bb, LANES), lambda i: (ro + i, gate_block)),
                pl.BlockSpec((4, 3 * W), lambda i: (0, 0)),
                pl.BlockSpec((3, bb, 3 * W), lambda i: (0, i, 0)),
                state_spec,
                pad_spec, pad_spec, pad_spec,
                pl.BlockSpec(memory_space=pl.ANY)]
    args = [qkvz, qkvz, qkvz, qkvz, uba, conv_w, conv0_t, s0_all, alog_pad2, dtb_pad2, d_norm, ymix]
    aliases = {11: 0}
    if snew_all is not None:
        in_specs.append(pl.BlockSpec(memory_space=pl.ANY))
        args.append(snew_all)
        aliases[12] = 1
    out_specs = [pl.BlockSpec((bb, W), lambda i: (ro + i, 0)),
                 state_spec,
                 pl.BlockSpec((3, bb, 3 * W), lambda i: (0, i, 0))]
    out_shape = [jax.ShapeDtypeStruct(ymix.shape, ymix.dtype),
                 jax.ShapeDtypeStruct(s0_all.shape, F32),
                 jax.ShapeDtypeStruct((3, n_seq, 3 * W), F32)]
    kern = functools.partial(_delta_decode_kernel, n_heads=H, bb=bb, has_prev=snew_all is not None)
    head_scr = pltpu.VMEM((H, bb, HEAD_DIM), F32)
    return pl.pallas_call(
        kern,
        grid=(n_seq // bb,),
        in_specs=in_specs,
        out_specs=out_specs,
        out_shape=out_shape,
        scratch_shapes=[head_scr, head_scr, head_scr, head_scr, pltpu.VMEM((bb, LANES), F32)],
        input_output_aliases=aliases,
        compiler_params=_cparams(1),
        name="delta_decode",
    )(*args)


def _s5_prep_kernel(are_ref, aim_ref, ls_ref, bre_ref, bim_ref,
                    lbre_ref, lbim_ref, bbre_ref, bbim_ref, *, n_state, group):
    a_re = are_ref[...]
    a_im = aim_ref[...]
    dt = jnp.exp(ls_ref[...])
    mag = jnp.exp(a_re * dt)
    ang = a_im * dt
    lb_re = mag * jnp.cos(ang)
    lb_im = mag * jnp.sin(ang)
    den = a_re * a_re + a_im * a_im
    num_re = lb_re - 1.0
    num_im = lb_im
    f_re = (num_re * a_re + num_im * a_im) / den
    f_im = (num_im * a_re - num_re * a_im) / den
    lbre_ref[...] = lb_re
    lbim_ref[...] = lb_im
    r = lax.broadcasted_iota(jnp.int32, (n_state, n_state * group), 0)
    cidx = lax.broadcasted_iota(jnp.int32, (n_state, n_state * group), 1)
    expand = jnp.where(_idiv_pow2(cidx, group) == r, 1.0, 0.0).astype(F32)
    fre_x = _dot(f_re, expand, HIGHEST)
    fim_x = _dot(f_im, expand, HIGHEST)
    b_re = bre_ref[...]
    b_im = bim_ref[...]
    bbre_ref[...] = fre_x * b_re - fim_x * b_im
    bbim_ref[...] = fre_x * b_im + fim_x * b_re


def s5_prep(a_re, a_im, log_step, b_re, b_im):
    g, n = a_re.shape
    group = b_re.shape[-1]
    full = lambda shape: pl.BlockSpec(shape, lambda: (0,) * len(shape))
    kern = functools.partial(_s5_prep_kernel, n_state=n, group=group)
    lb_re, lb_im, bb_re, bb_im = pl.pallas_call(
        kern,
        in_specs=[full((g, n)), full((g, n)), full((g, 1)), full((g, n * group)), full((g, n * group))],
        out_specs=[full((g, n)), full((g, n)), full((g, n * group)), full((g, n * group))],
        out_shape=[jax.ShapeDtypeStruct((g, n), F32), jax.ShapeDtypeStruct((g, n), F32),
                   jax.ShapeDtypeStruct((g, n * group), F32), jax.ShapeDtypeStruct((g, n * group), F32)],
        compiler_params=pltpu.CompilerParams(vmem_limit_bytes=VMEM_LIMIT_BYTES),
        name="s5_prep",
    )(a_re, a_im, log_step.reshape(g, 1), b_re.reshape(g, n * group), b_im.reshape(g, n * group))
    return lb_re, lb_im, bb_re.reshape(g, n, group), bb_im.reshape(g, n, group)


def _s5_decode_kernel(u_ref, h0r_ref, h0i_ref, lbr_ref, lbi_ref, wbr_ref, wbi_ref, wc_ref,
                      d_ref, wg_ref, bg_ref, ymix_any,
                      y_ref, hr_out, hi_out,
                      xr_scr, xi_scr, yc_scr, *, n_blocks, blk_in, blk_state):
    del ymix_any
    u = u_ref[...]
    u16 = u.astype(BF16)
    for j in range(n_blocks):
        uj = u16[:, j * blk_in:(j + 1) * blk_in]
        xr_scr[:, j * blk_state:(j + 1) * blk_state] = _dot(uj, wbr_ref[j])
        xi_scr[:, j * blk_state:(j + 1) * blk_state] = _dot(uj, wbi_ref[j])

    n_lanes = n_blocks * blk_state
    lane_chunk = _pick_tile(n_lanes, 1024, LANES)
    for lc in range(n_lanes // lane_chunk):
        ls = slice(lc * lane_chunk, (lc + 1) * lane_chunk)
        lbr, lbi = lbr_ref[:, ls], lbi_ref[:, ls]
        hr, hi = h0r_ref[:, ls], h0i_ref[:, ls]
        xr = lbr * hr - lbi * hi + xr_scr[:, ls]
        xi = lbr * hi + lbi * hr + xi_scr[:, ls]
        xr_scr[:, ls] = xr
        xi_scr[:, ls] = xi
        hr_out[:, ls] = xr
        hi_out[:, ls] = xi

    for j in range(n_blocks):
        ss = slice(j * blk_state, (j + 1) * blk_state)
        yj = _dot(xr_scr[:, ss].astype(BF16), wc_ref[j, :blk_state, :])
        yj = yj + _dot(xi_scr[:, ss].astype(BF16), wc_ref[j, blk_state:, :])
        yc_scr[:, j * blk_in:(j + 1) * blk_in] = yj
    y = _gelu_tanh(yc_scr[...] + d_ref[...] * u)
    gate = _dot(y.astype(BF16), wg_ref[...]) + bg_ref[...]
    y_ref[...] = (y * _sigmoid(gate)).astype(y_ref.dtype)


def s5_mixer_decode(uba, h0_re, h0_im, lb_re, lb_im, wb_re, wb_im, wc, d, w_glu16, b_glu, ymix, *,
                    row_off, n_groups, rb, width, col_block):
    n_blocks, blk_in, blk_state = wb_re.shape
    n_lanes = n_blocks * blk_state
    assert row_off % rb == 0 and rb % (2 * SUBLANES) == 0
    ro = row_off // rb

    def const(shape):
        return pl.BlockSpec(shape, lambda s: (0,) * len(shape))

    state_spec = pl.BlockSpec((rb, n_lanes), lambda s: (s, 0))
    in_specs = [pl.BlockSpec((rb, width), lambda s: (ro + s, 0)),
                state_spec, state_spec,
                const((1, n_lanes)), const((1, n_lanes)),
                const(wb_re.shape), const(wb_im.shape), const(wc.shape),
                const((1, width)), const(w_glu16.shape), const((1, width)),
                pl.BlockSpec(memory_space=pl.ANY)]
    out_specs = [pl.BlockSpec((rb, width), lambda s: (ro + s, col_block)), state_spec, state_spec]
    out_shape = [jax.ShapeDtypeStruct(ymix.shape, ymix.dtype),
                 jax.ShapeDtypeStruct(h0_re.shape, F32),
                 jax.ShapeDtypeStruct(h0_im.shape, F32)]
    kern = functools.partial(_s5_decode_kernel, n_blocks=n_blocks, blk_in=blk_in, blk_state=blk_state)
    return pl.pallas_call(
        kern,
        grid=(n_groups,),
        in_specs=in_specs,
        out_specs=out_specs,
        out_shape=out_shape,
        scratch_shapes=[pltpu.VMEM((rb, n_lanes), F32), pltpu.VMEM((rb, n_lanes), F32),
                        pltpu.VMEM((rb, width), F32)],
        input_output_aliases={11: 0},
        compiler_params=_cparams(1),
        name="s5_mixer_decode",
    )(uba, h0_re, h0_im, lb_re.reshape(1, n_lanes), lb_im.reshape(1, n_lanes),
      wb_re, wb_im, wc, d.reshape(1, width), w_glu16, b_glu.reshape(1, width), ymix)


def _s5_prompt_kernel(u_ref, h0r_ref, h0i_ref, lbr_ref, lbi_ref, wbr_ref, wbi_ref, wc_ref,
                      d_ref, wg_ref, bg_ref, ymix_any,
                      y_ref, hr_out, hi_out,
                      xr_scr, xi_scr, hr_scr, hi_scr, yc_scr, *, rb, n_blocks, blk_in, blk_state):
    del ymix_any
    t_blk = pl.program_id(0)
    nt = pl.num_programs(0)
    R = SUBLANES
    half = R // 2
    ng = rb // R
    nb2 = n_blocks // 2
    n_fold = nb2 * blk_state

    @pl.when(t_blk == 0)
    def _init():
        hr_scr[...] = h0r_ref[...]
        hi_scr[...] = h0i_ref[...]

    u = u_ref[...]
    low_in = lax.broadcasted_iota(jnp.int32, (ng, R, blk_in), 1) < half
    for j in range(nb2):
        ua = u[:, j * blk_in:(j + 1) * blk_in].reshape(ng, R, blk_in)
        ub = u[:, (j + nb2) * blk_in:(j + nb2 + 1) * blk_in].reshape(ng, R, blk_in)
        ra = pltpu.roll(ua, half, 1)
        rbk = pltpu.roll(ub, half, 1)
        a_even, a_odd = jnp.where(low_in, ua, 0.0), jnp.where(low_in, ra, 0.0)
        b_even, b_odd = jnp.where(low_in, 0.0, rbk), jnp.where(low_in, 0.0, ub)
        lhs_a = jnp.stack([a_even, a_odd], axis=1).reshape(2 * ng, R, blk_in)
        lhs_b = jnp.stack([b_even, b_odd], axis=1).reshape(2 * ng, R, blk_in)
        lhs = _b16(jnp.concatenate([lhs_a, lhs_b], axis=-1).reshape(2 * rb, 2 * blk_in))
        ss = slice(j * blk_state, (j + 1) * blk_state)
        xr_scr[:, :, ss] = _dot(lhs, wbr_ref[j]).reshape(2 * ng, R, blk_state)
        xi_scr[:, :, ss] = _dot(lhs, wbi_ref[j]).reshape(2 * ng, R, blk_state)

    lane_chunk = _pick_tile(n_fold, 1024, LANES)
    for lc in range(n_fold // lane_chunk):
        ls = slice(lc * lane_chunk, (lc + 1) * lane_chunk)
        low = lax.broadcasted_iota(jnp.int32, (R, lane_chunk), 0) < half
        l0 = slice(lc * lane_chunk, (lc + 1) * lane_chunk)
        l1 = slice(n_fold + lc * lane_chunk, n_fold + (lc + 1) * lane_chunk)
        lbr = jnp.where(low, lbr_ref[:, l0], lbr_ref[:, l1])
        lbi = jnp.where(low, lbi_ref[:, l0], lbi_ref[:, l1])

        def step(t, carry):
            hr, hi = carry
            xr = lbr * hr - lbi * hi + xr_scr[t, :, ls]
            xi = lbr * hi + lbi * hr + xi_scr[t, :, ls]
            xr_scr[t, :, ls] = xr
            xi_scr[t, :, ls] = xi
            return xr, xi

        hr, hi = lax.fori_loop(0, 2 * ng, step, (hr_scr[:, ls], hi_scr[:, ls]))
        hr_scr[:, ls] = hr
        hi_scr[:, ls] = hi

    @pl.when(t_blk == nt - 1)
    def _fin():
        hr_out[...] = hr_scr[...]
        hi_out[...] = hi_scr[...]

    low_st = lax.broadcasted_iota(jnp.int32, (ng, R, blk_state), 1) < half

    def unfold(x):
        x4 = x.reshape(ng, 2, R, blk_state)
        even, odd = x4[:, 0], x4[:, 1]
        f0 = jnp.where(low_st, even, pltpu.roll(odd, half, 1)).reshape(rb, blk_state)
        f1 = jnp.where(low_st, pltpu.roll(even, half, 1), odd).reshape(rb, blk_state)
        return _b16(f0), _b16(f1)

    for j in range(nb2):
        ss = slice(j * blk_state, (j + 1) * blk_state)
        r0, r1 = unfold(xr_scr[:, :, ss])
        i0, i1 = unfold(xi_scr[:, :, ss])
        j1 = j + nb2
        yc_scr[:, j * blk_in:(j + 1) * blk_in] = (
            _dot(r0, wc_ref[j, :blk_state, :]) + _dot(i0, wc_ref[j, blk_state:, :]))
        yc_scr[:, j1 * blk_in:(j1 + 1) * blk_in] = (
            _dot(r1, wc_ref[j1, :blk_state, :]) + _dot(i1, wc_ref[j1, blk_state:, :]))
    y = _gelu_tanh(yc_scr[...] + d_ref[...] * u)
    gate = _dot(y.astype(BF16), wg_ref[...]) + bg_ref[...]
    y_ref[...] = (y * _sigmoid(gate)).astype(y_ref.dtype)


def s5_mixer_prompt(uba, h0_re, h0_im, lb_re, lb_im, wb2_re, wb2_im, wc, d, w_glu16, b_glu, ymix, *,
                    n_time_blocks, rb, width, col_block):
    nb2, blk_in2, blk_state = wb2_re.shape
    blk_in = blk_in2 // 2
    n_blocks = 2 * nb2
    n_lanes = n_blocks * blk_state
    n_fold = n_lanes // 2
    assert rb % (2 * SUBLANES) == 0

    def const(shape):
        return pl.BlockSpec(shape, lambda t: (0,) * len(shape))

    state_spec = const((SUBLANES, n_fold))
    in_specs = [pl.BlockSpec((rb, width), lambda t: (t, 0)),
                state_spec, state_spec,
                const((1, n_lanes)), const((1, n_lanes)),
                const(wb2_re.shape), const(wb2_im.shape), const(wc.shape),
                const((1, width)), const(w_glu16.shape), const((1, width)),
                pl.BlockSpec(memory_space=pl.ANY)]
    out_specs = [pl.BlockSpec((rb, width), lambda t: (t, col_block)), state_spec, state_spec]
    out_shape = [jax.ShapeDtypeStruct(ymix.shape, ymix.dtype),
                 jax.ShapeDtypeStruct(h0_re.shape, F32),
                 jax.ShapeDtypeStruct(h0_im.shape, F32)]
    kern = functools.partial(_s5_prompt_kernel, rb=rb, n_blocks=n_blocks, blk_in=blk_in,
                             blk_state=blk_state)
    scan_scr = pltpu.VMEM((2 * rb // SUBLANES, SUBLANES, n_fold), F32)
    return pl.pallas_call(
        kern,
        grid=(n_time_blocks,),
        in_specs=in_specs,
        out_specs=out_specs,
        out_shape=out_shape,
        scratch_shapes=[scan_scr, scan_scr,
                        pltpu.VMEM((SUBLANES, n_fold), F32), pltpu.VMEM((SUBLANES, n_fold), F32),
                        pltpu.VMEM((rb, width), F32)],
        input_output_aliases={11: 0},
        compiler_params=_cparams(1),
        name="s5_mixer_prompt",
    )(uba, h0_re, h0_im, lb_re.reshape(1, n_lanes), lb_im.reshape(1, n_lanes),
      wb2_re, wb2_im, wc, d.reshape(1, width), w_glu16, b_glu.reshape(1, width), ymix)


def _s5_block_weights(bb_re, bb_im, c_re, c_im):
    g, n, grp = bb_re.shape
    r = S5_GROUPS_PER_BLOCK
    nb = g // r
    eye = jnp.eye(r, dtype=F32)

    def in_blocks(bb):
        w = jnp.einsum("jsnc,rs->jrcsn", bb.reshape(nb, r, n, grp), eye)
        return w.reshape(nb, r * grp, r * n).astype(BF16)

    def out_blocks(cc):
        w = jnp.einsum("jrcn,rs->jsnrc", cc.reshape(nb, r, grp, n), eye)
        return w.reshape(nb, r * n, r * grp)

    wc = jnp.concatenate([out_blocks(c_re), -out_blocks(c_im)], axis=1).astype(BF16)
    return in_blocks(bb_re), in_blocks(bb_im), wc


def kernel(x_prompt, x_sample, state_delta, state_conv, state_s5_re, state_s5_im, norm_mix, w_in, conv_w, delta_a_log, delta_dt_bias, delta_norm, s5_a_re, s5_a_im, s5_log_step, s5_b_re, s5_b_im, s5_c_re, s5_c_im, s5_d, s5_w_glu, s5_b_glu, w_out, norm_ffn, w_ffn_in, w_ffn_out, norm_final):
    B, L, D = x_prompt.shape
    Bs = x_sample.shape[0]
    depth = w_in.shape[0]
    H = delta_a_log.shape[1]
    Wd = H * HEAD_DIM
    G, N = s5_a_re.shape[1:]
    grp = s5_b_re.shape[-1]
    Ws = G * grp
    d_ff = w_ffn_out.shape[1]
    Mp = B * L
    M = Mp + Bs
    half = SUBLANES // 2
    assert B == half and x_sample.shape[1] == 1 and Wd % Ws == 0
    uba_w = Ws + GATE_TILES * LANES

    x = jnp.concatenate([jnp.transpose(x_prompt, (1, 0, 2)).reshape(Mp, D),
                         x_sample.reshape(Bs, D)], axis=0)

    gate_cols = 4 * Wd
    gain_mix = norm_mix[:, :, None]
    w_qkvz16 = (w_in[:, :, :gate_cols] * gain_mix).astype(BF16)
    w_uba = (jnp.concatenate([w_in[:, :, gate_cols + 2 * H:], w_in[:, :, gate_cols:gate_cols + 2 * H],
                              jnp.zeros((depth, D, GATE_TILES * LANES - 2 * H), F32)], axis=2)
             * gain_mix).astype(BF16)
    gain_ffn = jnp.broadcast_to(norm_ffn[:, :, None], (depth, D, LANES))
    pad_h = lambda a: jnp.pad(a, (0, LANES - H)).reshape(1, LANES)
    pad_h2 = lambda a: jnp.pad(a, (H, LANES - 2 * H)).reshape(1, LANES)
    zero_s = jnp.zeros((B, H, HEAD_DIM, HEAD_DIM), F32)
    zero_c = jnp.zeros((B, 3, 3 * Wd), F32)
    zero_h = jnp.zeros((SUBLANES, G * N // 2), F32)
    s5_rb = _pick_tile(Mp, 128, 2 * SUBLANES)
    s5_rb_dec = _pick_tile(Bs, 32, 2 * SUBLANES)

    w_ffn_out16 = cast_bf16(w_ffn_out)
    s_dec = None
    outs = {k: [] for k in ("dp", "cp", "rp", "ip", "cs", "rs", "is")}
    x16, ssq = row_stats(x)
    for l in range(depth):
        qkvz = matmul(x16, w_qkvz16, l, n=gate_cols, tn=1024, row_ssq=ssq)
        uba = matmul(x16, w_uba, l, n=uba_w, tn=640, row_ssq=ssq)

        ymix = jnp.zeros((M, Wd + Ws), BF16)
        dnorm = delta_norm[l].reshape(1, HEAD_DIM)
        ymix, s_p, cnq, cnk, cnv = delta_prompt(
            qkvz, uba, conv_w[l], zero_c, zero_s, pad_h(delta_a_log[l]), pad_h(delta_dt_bias[l]),
            dnorm, ymix, seq=L, batch=B, n_heads=H)
        ymix, s_dec, cn_s = delta_decode(
            qkvz, uba, conv_w[l], jnp.transpose(state_conv[l], (1, 0, 2)), state_delta, l,
            pad_h2(delta_a_log[l]), pad_h2(delta_dt_bias[l]), dnorm, ymix, s_dec,
            row_off=Mp, n_seq=Bs, n_heads=H)
        outs["dp"].append(s_p)
        outs["cp"].append(jnp.concatenate([cnq, cnk, cnv], axis=-1))
        outs["cs"].append(jnp.transpose(cn_s, (1, 0, 2)))

        lb_re, lb_im, bb_re, bb_im = s5_prep(s5_a_re[l], s5_a_im[l], s5_log_step[l],
                                             s5_b_re[l], s5_b_im[l])
        wb_re, wb_im, wc = _s5_block_weights(bb_re, bb_im, s5_c_re[l], s5_c_im[l])
        w_glu16 = s5_w_glu[l].astype(BF16)
        nb2 = wb_re.shape[0] // 2
        ymix, hr_p, hi_p = s5_mixer_prompt(
            uba, zero_h, zero_h, lb_re, lb_im,
            jnp.concatenate([wb_re[:nb2], wb_re[nb2:]], axis=1),
            jnp.concatenate([wb_im[:nb2], wb_im[nb2:]], axis=1),
            wc, s5_d[l], w_glu16, s5_b_glu[l], ymix,
            n_time_blocks=Mp // s5_rb, rb=s5_rb, width=Ws, col_block=Wd // Ws)
        ymix, hr_s, hi_s = s5_mixer_decode(
            uba, state_s5_re[l].reshape(Bs, G * N), state_s5_im[l].reshape(Bs, G * N),
            lb_re, lb_im, wb_re, wb_im, wc, s5_d[l], w_glu16, s5_b_glu[l], ymix,
            row_off=Mp, n_groups=Bs // s5_rb_dec, rb=s5_rb_dec, width=Ws, col_block=Wd // Ws)
        unfold = lambda a: jnp.transpose(a.reshape(2, B, G * N // 2), (1, 0, 2)).reshape(B, G, N)
        outs["rp"].append(unfold(hr_p))
        outs["ip"].append(unfold(hi_p))
        outs["rs"].append(hr_s.reshape(Bs, G, N))
        outs["is"].append(hi_s.reshape(Bs, G, N))

        x, x16, ssq = matmul(ymix, w_out, l, n=D, res=x, rows_outer=True, emit_stats=True)

        hidden = matmul(x16, w_ffn_in, l, n=d_ff, col_off2=d_ff, out_dtype=BF16, tm=2080, tn=256,
                        rows_outer=True, x_single_buffer=True, row_ssq=ssq, w_scale=gain_ffn)
        x, x16, ssq = matmul(hidden, w_ffn_out16, l, n=D, res=x, tm=520, tn=512, rows_outer=True,
                             emit_stats=True)

    y = rmsnorm_rows(x, norm_final, F32)
    y_prompt = jnp.transpose(y[:Mp].reshape(L, B, D), (1, 0, 2))
    y_sample = y[Mp:].reshape(Bs, 1, D)
    st = lambda k: jnp.stack(outs[k])
    return (y_prompt, y_sample, st("dp"), st("cp"), st("rp"), st("ip"),
            s_dec, st("cs"), st("rs"), st("is"))
```

```python
import functools

import jax
import jax.numpy as jnp
from jax import lax
from jax.experimental import pallas as pl
from jax.experimental.pallas import tpu as pltpu

F32 = jnp.float32
BF16 = jnp.bfloat16
EPS = 1e-6

LANES = 128
SUBLANES = 8
VMEM_LIMIT_BYTES = 56 * 1024 * 1024
HEAD_DIM = 128
DELTA_CHUNK = 128
SOLVE_BLOCK = 16
S5_GROUPS_PER_BLOCK = 8
GATE_TILES = 4
HIGHEST = lax.Precision.HIGHEST


def _cparams(n_axes):
    return pltpu.CompilerParams(dimension_semantics=("arbitrary",) * n_axes,
                                vmem_limit_bytes=VMEM_LIMIT_BYTES)


def _pick_tile(n, target, mult):
    best = None
    for d in range(mult, min(n, target) + 1, mult):
        if n % d == 0:
            best = d
    return n if best is None else best


def _idiv_pow2(x, c):
    shift = c.bit_length() - 1
    assert c == 1 << shift
    return lax.shift_right_logical(x, jnp.int32(shift))


def _sigmoid(x):
    return 1.0 / (1.0 + jnp.exp(-x))


def _silu(x):
    return x * _sigmoid(x)


def _softplus(x):
    return jnp.maximum(x, 0.0) + jnp.log(1.0 + jnp.exp(-jnp.abs(x)))


def _gelu_tanh(x):
    c = 0.7978845608028654
    return 0.5 * x * (1.0 + jnp.tanh(c * (x + 0.044715 * (x * x * x))))


def _dot(a, b, precision=None):
    return jnp.dot(a, b, preferred_element_type=F32, precision=precision)


def _dot_nt(a, b, precision=None):
    return lax.dot_general(a, b, (((1,), (1,)), ((), ())), preferred_element_type=F32,
                           precision=precision)


def _dot_tn(a, b, precision=None):
    return lax.dot_general(a, b, (((0,), (0,)), ((), ())), preferred_element_type=F32,
                           precision=precision)


def _b16(x):
    return x.astype(BF16)


def _rmsnorm_kernel(x_ref, w_ref, o_ref):
    x = x_ref[...]
    y = x * lax.rsqrt(jnp.mean(x * x, axis=-1, keepdims=True) + EPS)
    o_ref[...] = (y * w_ref[...]).astype(o_ref.dtype)


def rmsnorm_rows(x, w, out_dtype):
    m, d = x.shape
    tr = _pick_tile(m, 512, 16)
    return pl.pallas_call(
        _rmsnorm_kernel,
        grid=(m // tr,),
        in_specs=[pl.BlockSpec((tr, d), lambda i: (i, 0)),
                  pl.BlockSpec((1, d), lambda i: (0, 0))],
        out_specs=pl.BlockSpec((tr, d), lambda i: (i, 0)),
        out_shape=jax.ShapeDtypeStruct((m, d), out_dtype),
        compiler_params=_cparams(1),
        name="rmsnorm_rows",
    )(x, w.reshape(1, d))


def _interleave4(xs):
    t, d = xs[0].shape
    ng = t // SUBLANES
    sub = lax.broadcasted_iota(jnp.int32, (ng, SUBLANES, d), 1)
    x3 = [x.reshape(ng, SUBLANES, d) for x in xs]
    outs = []
    for q in range(4):
        acc = jnp.zeros((ng, SUBLANES, d), xs[0].dtype)
        for b in range(4):
            first = pltpu.roll(x3[b], (b - 2 * q) % SUBLANES, 1)
            second = pltpu.roll(x3[b], (3 + b - 2 * q) % SUBLANES, 1)
            acc = jnp.where(sub == b, first, jnp.where(sub == 4 + b, second, acc))
        outs.append(acc)
    return jnp.stack(outs, axis=1).reshape(4 * t, d)


def _deinterleave4(y):
    rows, d = y.shape
    t = rows // 4
    ng = t // SUBLANES
    sub = lax.broadcasted_iota(jnp.int32, (ng, SUBLANES, d), 1)
    y4 = y.reshape(ng, 4, SUBLANES, d)
    outs = []
    for b in range(4):
        acc = jnp.zeros((ng, SUBLANES, d), y.dtype)
        for q in range(4):
            yq = y4[:, q]
            first = pltpu.roll(yq, (2 * q - b) % SUBLANES, 1)
            second = pltpu.roll(yq, (2 * q - 3 - b) % SUBLANES, 1)
            acc = jnp.where(sub == 2 * q, first, jnp.where(sub == 2 * q + 1, second, acc))
        outs.append(acc.reshape(t, d))
    return outs


def _to_rows_kernel(xp_ref, xs_ref, o_ref, *, n_prompt_blocks):
    i = pl.program_id(0)

    @pl.when(i < n_prompt_blocks)
    def _prompt():
        o_ref[...] = _interleave4([xp_ref[b] for b in range(4)])

    @pl.when(i >= n_prompt_blocks)
    def _decode():
        o_ref[...] = xs_ref[...]


def to_rows(x_prompt, x_sample2d):
    b, l, d = x_prompt.shape
    bs = x_sample2d.shape[0]
    assert b == 4 and bs % (4 * SUBLANES) == 0 and (4 * l) % bs == 0
    tl = bs // 4
    npb = l // tl
    kern = functools.partial(_to_rows_kernel, n_prompt_blocks=npb)
    return pl.pallas_call(
        kern,
        grid=(npb + 1,),
        in_specs=[pl.BlockSpec((4, tl, d), lambda i: (0, jnp.minimum(i, npb - 1), 0)),
                  pl.BlockSpec((bs, d), lambda i: (0, 0))],
        out_specs=pl.BlockSpec((bs, d), lambda i: (i, 0)),
        out_shape=jax.ShapeDtypeStruct((4 * l + bs, d), x_prompt.dtype),
        compiler_params=_cparams(1),
        name="to_rows",
    )(x_prompt, x_sample2d)


def _final_norm_kernel(x_ref, w_ref, yp_ref, ys_ref, *, n_prompt_blocks):
    i = pl.program_id(0)
    x = x_ref[...]
    y = x * lax.rsqrt(jnp.mean(x * x, axis=-1, keepdims=True) + EPS) * w_ref[...]

    @pl.when(i < n_prompt_blocks)
    def _prompt():
        for b, yb in enumerate(_deinterleave4(y)):
            yp_ref[b] = yb

    @pl.when(i >= n_prompt_blocks)
    def _decode():
        ys_ref[...] = y


def final_norm(x, w, *, batch, seq, n_decode):
    m, d = x.shape
    bs = n_decode
    assert batch == 4 and m == 4 * seq + bs and bs % (4 * SUBLANES) == 0 and (4 * seq) % bs == 0
    tl = bs // 4
    npb = seq // tl
    kern = functools.partial(_final_norm_kernel, n_prompt_blocks=npb)
    return pl.pallas_call(
        kern,
        grid=(npb + 1,),
        in_specs=[pl.BlockSpec((bs, d), lambda i: (i, 0)),
                  pl.BlockSpec((1, d), lambda i: (0, 0))],
        out_specs=[pl.BlockSpec((4, tl, d), lambda i: (0, jnp.minimum(i, npb - 1), 0)),
                   pl.BlockSpec((bs, d), lambda i: (0, 0))],
        out_shape=[jax.ShapeDtypeStruct((4, seq, d), x.dtype), jax.ShapeDtypeStruct((bs, d), x.dtype)],
        compiler_params=_cparams(1),
        name="final_norm",
    )(x, w.reshape(1, d))


def _mm_kernel(*refs, n_w, has_res):
    x_ref = refs[0]
    w_refs = refs[1:1 + n_w]
    res_ref = refs[1 + n_w] if has_res else None
    o_ref = refs[-1]
    x = x_ref[...]
    parts = [_dot(x, _b16(w_ref[...])) for w_ref in w_refs]
    y = parts[0]
    if n_w == 2:
        y = _silu(y) * parts[1]
    if has_res:
        y = y + res_ref[...]
    o_ref[...] = y.astype(o_ref.dtype)


def matmul(x, w, layer, *, n, col_off=0, col_off2=None, res=None, out_dtype=F32,
           tm=1040, tn=512, rows_outer=False, x_single_buffer=False):
    m, k_total = x.shape
    assert w.shape[1] == k_total
    tm = _pick_tile(m, tm, 16)
    tn = _pick_tile(n, tn, LANES)
    assert col_off % tn == 0 and (col_off2 is None or col_off2 % tn == 0)
    n_w = 1 if col_off2 is None else 2
    if rows_outer:
        grid = (m // tm, n // tn)
        ij = lambda a, b: (a, b)
    else:
        grid = (n // tn, m // tm)
        ij = lambda a, b: (b, a)
    x_mode = dict(pipeline_mode=pl.Buffered(1)) if (x_single_buffer and rows_outer) else {}
    in_specs = [pl.BlockSpec((tm, k_total), lambda a, b: (ij(a, b)[0], 0), **x_mode)]
    args = [x]
    for off in (col_off, col_off2)[:n_w]:
        in_specs.append(pl.BlockSpec((None, k_total, tn),
                                     lambda a, b, o=off // tn: (layer, 0, ij(a, b)[1] + o)))
        args.append(w)
    if res is not None:
        in_specs.append(pl.BlockSpec((tm, tn), lambda a, b: ij(a, b)))
        args.append(res)
    kern = functools.partial(_mm_kernel, n_w=n_w, has_res=res is not None)
    return pl.pallas_call(
        kern,
        grid=grid,
        in_specs=in_specs,
        out_specs=pl.BlockSpec((tm, tn), lambda a, b: ij(a, b)),
        out_shape=jax.ShapeDtypeStruct((m, n), out_dtype),
        compiler_params=_cparams(2),
        name="matmul_rows",
    )(*args)


def _cast_kernel(x_ref, o_ref):
    o_ref[...] = x_ref[...].astype(o_ref.dtype)


def cast_bf16(w):
    depth, k, n = w.shape
    tr = _pick_tile(k, 512, 16)
    return pl.pallas_call(
        _cast_kernel,
        grid=(depth, k // tr),
        in_specs=[pl.BlockSpec((None, tr, n), lambda l, i: (l, i, 0))],
        out_specs=pl.BlockSpec((None, tr, n), lambda l, i: (l, i, 0)),
        out_shape=jax.ShapeDtypeStruct(w.shape, BF16),
        compiler_params=_cparams(2),
        name="cast_bf16",
    )(w)


def _lane_pick(x, idx):
    lane = lax.broadcasted_iota(jnp.int32, x.shape, 1)
    return jnp.sum(jnp.where(lane == idx, x, 0.0), axis=-1, keepdims=True)


def _unit_lower_inverse_minus_eye(a_list):
    C = DELTA_CHUNK
    assert C == 8 * SOLVE_BLOCK
    row = lax.broadcasted_iota(jnp.int32, (C, C), 0)
    col = lax.broadcasted_iota(jnp.int32, (C, C), 1)
    same_block = _idiv_pow2(row, SOLVE_BLOCK) == _idiv_pow2(col, SOLVE_BLOCK)
    d = [jnp.where(same_block, a, 0.0) for a in a_list]
    e = [a - dd for a, dd in zip(a_list, d)]
    p = [-dd for dd in d]
    td = list(p)
    for _ in range(3):
        p = [_dot(_b16(x), _b16(x)) for x in p]
        td = [t + x + _dot(_b16(t), _b16(x)) for t, x in zip(td, p)]
    f = [ee + _dot(_b16(t), _b16(ee)) for t, ee in zip(td, e)]
    f2 = [_dot(_b16(x), _b16(x)) for x in f]
    f4 = [_dot(_b16(x), _b16(x)) for x in f2]
    h1 = [x2 - x - _dot(_b16(x), _b16(x2)) for x, x2 in zip(f, f2)]
    gd = [h + x4 + _dot(_b16(h), _b16(x4)) for h, x4 in zip(h1, f4)]
    return [g + t + _dot(_b16(g), _b16(t)) for g, t in zip(gd, td)]


def _delta_prompt_kernel(q_ref, k_ref, v_ref, z_ref, ba_ref, cwq_ref, cwk_ref, cwv_ref,
                         c0q_ref, c0k_ref, c0v_ref, s0_ref, alog_ref, dtb_ref, dn_ref,
                         y_ref, sfin_ref, cnq_ref, cnk_ref, cnv_ref,
                         s_scr, xbuf, ybuf, *, n_heads, lc, batch):
    h = pl.program_id(0)
    c = pl.program_id(1)
    nc = pl.num_programs(1)
    C = DELTA_CHUNK
    B = batch
    n_chunks = lc // C
    PAD = SUBLANES

    def rows_of(ref, b):
        return ref[pl.ds(b, lc, stride=B), :]

    @pl.when(c == 0)
    def _init():
        s_scr[...] = s0_ref[...]
        for b in range(B):
            xbuf[0, b, PAD - 3:PAD, :] = c0q_ref[b]
            xbuf[1, b, PAD - 3:PAD, :] = c0k_ref[b]
            xbuf[2, b, PAD - 3:PAD, :] = c0v_ref[b]

    def conv_silu(j, b, x_ref, cw_ref, cn_ref):
        xbuf[j, b, PAD:PAD + lc, :] = rows_of(x_ref, b)
        cw = cw_ref[...]
        acc = xbuf[j, b, PAD - 3:PAD - 3 + lc, :] * cw[0:1, :]
        acc = acc + xbuf[j, b, PAD - 2:PAD - 2 + lc, :] * cw[1:2, :]
        acc = acc + xbuf[j, b, PAD - 1:PAD - 1 + lc, :] * cw[2:3, :]
        acc = acc + xbuf[j, b, PAD:PAD + lc, :] * cw[3:4, :]
        tail = xbuf[j, b, PAD + lc - 3:PAD + lc, :]
        xbuf[j, b, PAD - 3:PAD, :] = tail

        @pl.when(c == nc - 1)
        def _():
            cn_ref[b] = tail

        return _silu(acc)

    a_log = _lane_pick(alog_ref[...], h)
    dt_b = _lane_pick(dtb_ref[...], h)
    neg_a = -jnp.exp(a_log)

    crow = lax.broadcasted_iota(jnp.int32, (C, C), 0)
    ccol = lax.broadcasted_iota(jnp.int32, (C, C), 1)
    lane = lax.broadcasted_iota(jnp.int32, (C, LANES), 1)

    qs, ks, vs, betas, gmat = [], [], [], [], jnp.zeros((C, LANES), F32)
    for b in range(B):
        q = conv_silu(0, b, q_ref, cwq_ref, cnq_ref)
        k = conv_silu(1, b, k_ref, cwk_ref, cnk_ref)
        v = conv_silu(2, b, v_ref, cwv_ref, cnv_ref)
        q = q * lax.rsqrt(jnp.sum(q * q, axis=-1, keepdims=True) + EPS) * (HEAD_DIM ** -0.5)
        k = k * lax.rsqrt(jnp.sum(k * k, axis=-1, keepdims=True) + EPS)
        ba = rows_of(ba_ref, b)
        beta = _sigmoid(_lane_pick(ba, h))
        g = neg_a * _softplus(_lane_pick(ba, h + n_heads) + dt_b)
        for ci in range(n_chunks):
            sl = slice(ci * C, (ci + 1) * C)
            u = b * n_chunks + ci
            qs.append(q[sl]); ks.append(k[sl]); vs.append(v[sl]); betas.append(beta[sl])
            gmat = gmat + jnp.where(lane == u, g[sl], 0.0)
    n_units = B * n_chunks
    assert n_units <= LANES

    tri = jnp.where(crow >= ccol, 1.0, 0.0).astype(F32)
    gc_mat = _dot(tri, gmat, HIGHEST)
    gc_t = jnp.transpose(gc_mat)

    lower_incl = crow >= ccol
    lower_strict = crow > ccol

    def chunk_local(us):
        pick = lambda xs: [xs[u] for u in us]
        q_, k_, v_, b_ = pick(qs), pick(ks), pick(vs), pick(betas)
        gcs = [gc_mat[:, u:u + 1] for u in us]
        egc = [jnp.exp(x) for x in gcs]
        kb = [x * y for x, y in zip(k_, b_)]
        k16 = [_b16(x) for x in k_]
        kq = [_dot_nt(_b16(jnp.concatenate([x, y], axis=0)), z) for x, y, z in zip(kb, q_, k16)]
        decay_incl = [jnp.exp(jnp.where(lower_incl, g - gc_t[u:u + 1, :], -jnp.inf))
                      for g, u in zip(gcs, us)]
        a_strict = [jnp.where(lower_strict, x[:C] * d_, 0.0) for x, d_ in zip(kq, decay_incl)]
        attn16 = [_b16(x[C:] * d_) for x, d_ in zip(kq, decay_incl)]
        md = _unit_lower_inverse_minus_eye(a_strict)
        rhs = [jnp.concatenate([x * y, z * e], axis=-1)
               for x, y, z, e in zip(v_, b_, kb, egc)]
        sol = [r + _dot(_b16(m), _b16(r)) for m, r in zip(md, rhs)]
        gc_last = [x[C - 1:C, :] for x in gcs]
        kcum_qdec16 = [_b16(jnp.concatenate([s_[:, HEAD_DIM:], x * e], axis=0))
                       for s_, x, e in zip(sol, q_, egc)]
        kdec16 = [_b16(x * jnp.exp(gl - g)) for x, gl, g in zip(k_, gc_last, gcs)]
        g_last = [jnp.exp(x) for x in gc_last]
        value = [s_[:, :HEAD_DIM] for s_ in sol]
        return attn16, value, kcum_qdec16, kdec16, g_last

    attn16, value, kcum_qdec16, kdec16, g_last = chunk_local(list(range(n_units)))
    s = [s_scr[b] for b in range(B)]
    o_units = [None] * n_units
    for ci in range(n_chunks):
        us = [b * n_chunks + ci for b in range(B)]
        ks_qs = [_dot(kcum_qdec16[u], _b16(s[b])) for b, u in enumerate(us)]
        v_new16 = [_b16(value[u] - x[:C]) for u, x in zip(us, ks_qs)]
        for b, u in enumerate(us):
            o_units[u] = ks_qs[b][C:] + _dot(attn16[u], v_new16[b])
            s[b] = s[b] * g_last[u] + _dot_tn(kdec16[u], v_new16[b])
    for b in range(B):
        s_scr[b] = s[b]

    @pl.when(c == nc - 1)
    def _():
        for b in range(B):
            sfin_ref[b] = s[b]

    dn = dn_ref[...]
    for b in range(B):
        o = jnp.concatenate(o_units[b * n_chunks:(b + 1) * n_chunks], axis=0)
        o = o * lax.rsqrt(jnp.mean(o * o, axis=-1, keepdims=True) + EPS) * dn
        ybuf[pl.ds(b, lc, stride=B), :] = o * _silu(rows_of(z_ref, b))
    y_ref[...] = ybuf[...].astype(y_ref.dtype)


def delta_prompt(qkvz, uba, conv_w, conv0, s0, alog_pad, dtb_pad, d_norm, *,
                 seq, batch, n_heads, ymix_shape):
    H = n_heads
    B = batch
    lc = _pick_tile(seq, 512, DELTA_CHUNK)
    assert lc % DELTA_CHUNK == 0
    nblk = seq // lc
    gate_block = uba.shape[1] // LANES - GATE_TILES
    rb = lc * B

    def col_spec(part):
        return pl.BlockSpec((rb, LANES), lambda h, c: (c, part * H + h))

    def cw_spec(part):
        return pl.BlockSpec((4, LANES), lambda h, c: (0, part * H + h))

    def c0_spec(part):
        return pl.BlockSpec((B, 3, LANES), lambda h, c: (0, 0, part * H + h))

    pad_spec = pl.BlockSpec((1, LANES), lambda h, c: (0, 0))
    state_spec = pl.BlockSpec((B, None, HEAD_DIM, HEAD_DIM), lambda h, c: (0, h, 0, 0))
    in_specs = [col_spec(0), col_spec(1), col_spec(2), col_spec(3),
                pl.BlockSpec((rb, LANES), lambda h, c: (c, gate_block)),
                cw_spec(0), cw_spec(1), cw_spec(2),
                c0_spec(0), c0_spec(1), c0_spec(2),
                state_spec, pad_spec, pad_spec, pad_spec]
    cn_spec = pl.BlockSpec((B, 3, LANES), lambda h, c: (0, 0, h))
    out_specs = [pl.BlockSpec((rb, LANES), lambda h, c: (c, h)), state_spec, cn_spec, cn_spec, cn_spec]
    cn_shape = jax.ShapeDtypeStruct((B, 3, H * HEAD_DIM), F32)
    out_shape = [jax.ShapeDtypeStruct(ymix_shape, BF16),
                 jax.ShapeDtypeStruct((B, H, HEAD_DIM, HEAD_DIM), F32),
                 cn_shape, cn_shape, cn_shape]
    kern = functools.partial(_delta_prompt_kernel, n_heads=H, lc=lc, batch=B)
    return pl.pallas_call(
        kern,
        grid=(H, nblk),
        in_specs=in_specs,
        out_specs=out_specs,
        out_shape=out_shape,
        scratch_shapes=[pltpu.VMEM((B, HEAD_DIM, HEAD_DIM), F32),
                        pltpu.VMEM((3, B, SUBLANES + lc, LANES), F32),
                        pltpu.VMEM((rb, LANES), F32)],
        compiler_params=_cparams(2),
        name="delta_prompt",
    )(qkvz, qkvz, qkvz, qkvz, uba, conv_w, conv_w, conv_w,
      conv0, conv0, conv0, s0, alog_pad, dtb_pad, d_norm)


def _delta_decode_kernel(q_ref, k_ref, v_ref, z_ref, ba_ref, cw_ref, c0_ref, s0_ref,
                         alog_ref, dtb_ref, dn_ref, *rest, n_heads, bb, has_prev):
    n_any = 2 if has_prev else 1
    y_ref, snew_ref, cnew_ref, qn_scr, kn_scr, vn_scr, o_scr, bg_scr = rest[n_any:]
    H = n_heads
    W = H * HEAD_DIM
    cw = cw_ref[...]

    def conv_silu(part, x_ref):
        x = x_ref[...]
        sl = slice(part * W, (part + 1) * W)
        acc = c0_ref[0, :, sl] * cw[0:1, sl]
        acc = acc + c0_ref[1, :, sl] * cw[1:2, sl]
        acc = acc + c0_ref[2, :, sl] * cw[2:3, sl]
        acc = acc + x * cw[3:4, sl]
        cnew_ref[0, :, sl] = c0_ref[1, :, sl]
        cnew_ref[1, :, sl] = c0_ref[2, :, sl]
        cnew_ref[2, :, sl] = x
        return _silu(acc)

    q = conv_silu(0, q_ref)
    k = conv_silu(1, k_ref)
    v = conv_silu(2, v_ref)
    for h in range(H):
        hs = slice(h * HEAD_DIM, (h + 1) * HEAD_DIM)
        qh, kh = q[:, hs], k[:, hs]
        qn_scr[h] = qh * lax.rsqrt(jnp.sum(qh * qh, axis=-1, keepdims=True) + EPS) * (HEAD_DIM ** -0.5)
        kn_scr[h] = kh * lax.rsqrt(jnp.sum(kh * kh, axis=-1, keepdims=True) + EPS)
        vn_scr[h] = v[:, hs]

    ba = ba_ref[...]
    lane = lax.broadcasted_iota(jnp.int32, ba.shape, 1)
    beta_all = _sigmoid(ba)
    g_all = -jnp.exp(alog_ref[...]) * _softplus(ba + dtb_ref[...])
    bg_scr[...] = jnp.where(lane < H, beta_all, g_all)

    ROWS16 = 2 * SUBLANES

    def split(x):
        hi = _b16(x).astype(F32)
        return hi, _b16(x - hi).astype(F32)

    def stack16(rows):
        pad = jnp.zeros((ROWS16 - len(rows), HEAD_DIM), F32)
        return _b16(jnp.concatenate(list(rows) + [pad], axis=0))

    def per_seq(bi, carry):
        bg = bg_scr[pl.ds(bi, 1), :]
        heads = range(H)
        k_row = [kn_scr[h, pl.ds(bi, 1), :] for h in heads]
        q_row = [qn_scr[h, pl.ds(bi, 1), :] for h in heads]
        v_row = [vn_scr[h, pl.ds(bi, 1), :] for h in heads]
        beta = [bg[:, h:h + 1] for h in heads]
        eg = [jnp.exp(bg[:, H + h:H + h + 1]) for h in heads]
        s = [s0_ref[bi, h] for h in heads]
        k_sp = [split(x) for x in k_row]
        q_sp = [split(x) for x in q_row]
        s_sp = [split(x) for x in s]
        lhs = [stack16([kh, kl, qh, ql]) for (kh, kl), (qh, ql) in zip(k_sp, q_sp)]
        r_hi = [_dot(l_, _b16(sh)) for l_, (sh, _) in zip(lhs, s_sp)]
        r_lo = [_dot(l_, _b16(sl)) for l_, (_, sl) in zip(lhs, s_sp)]
        ks = [a[0:1] + a[1:2] + b[0:1] for a, b in zip(r_hi, r_lo)]
        qs = [a[2:3] + a[3:4] + b[2:3] for a, b in zip(r_hi, r_lo)]
        v_new = [b_ * v_ - (b_ * e_) * ks_ for b_, v_, e_, ks_ in zip(beta, v_row, eg, ks)]
        qk = [jnp.sum(q_ * k_, axis=-1, keepdims=True) for q_, k_ in zip(q_row, k_row)]
        o = [e_ * qs_ + qk_ * vn_ for e_, qs_, qk_, vn_ in zip(eg, qs, qk, v_new)]
        vn_sp = [split(x) for x in v_new]
        outer = [_dot_tn(stack16([kh, kh, kl]), stack16([vh, vl, vh]))
                 for (kh, kl), (vh, vl) in zip(k_sp, vn_sp)]
        for h in heads:
            snew_ref[bi, h] = s[h] * eg[h] + outer[h]
            o_scr[h, pl.ds(bi, 1), :] = o[h]
        return carry

    lax.fori_loop(0, bb, per_seq, 0)

    dn = dn_ref[...]
    z = z_ref[...]
    for h in range(H):
        hs = slice(h * HEAD_DIM, (h + 1) * HEAD_DIM)
        o = o_scr[h]
        o = o * lax.rsqrt(jnp.mean(o * o, axis=-1, keepdims=True) + EPS) * dn
        y_ref[:, hs] = (o * _silu(z[:, hs])).astype(y_ref.dtype)


def delta_decode(qkvz, uba, conv_w, conv0_t, s0_all, layer, alog_pad2, dtb_pad2, d_norm, ymix,
                 snew_all, *, row_off, n_seq, n_heads):
    H = n_heads
    W = H * HEAD_DIM
    bb = SUBLANES
    assert n_seq % bb == 0 and row_off % bb == 0
    ro = row_off // bb
    gate_block = uba.shape[1] // LANES - GATE_TILES

    def col_spec(part):
        return pl.BlockSpec((bb, W), lambda i: (ro + i, part))

    pad_spec = pl.BlockSpec((1, LANES), lambda i: (0, 0))
    state_spec = pl.BlockSpec((None, bb, H, HEAD_DIM, HEAD_DIM), lambda i: (layer, i, 0, 0, 0))
    in_specs = [col_spec(0), col_spec(1), col_spec(2), col_spec(3),
                pl.BlockSpec((bb, LANES), lambda i: (ro + i, gate_block)),
                pl.BlockSpec((4, 3 * W), lambda i: (0, 0)),
                pl.BlockSpec((3, bb, 3 * W), lambda i: (0, i, 0)),
                state_spec,
                pad_spec, pad_spec, pad_spec,
                pl.BlockSpec(memory_space=pl.ANY)]
    args = [qkvz, qkvz, qkvz, qkvz, uba, conv_w, conv0_t, s0_all, alog_pad2, dtb_pad2, d_norm, ymix]
    aliases = {11: 0}
    if snew_all is not None:
        in_specs.append(pl.BlockSpec(memory_space=pl.ANY))
        args.append(snew_all)
        aliases[12] = 1
    out_specs = [pl.BlockSpec((bb, W), lambda i: (ro + i, 0)),
                 state_spec,
                 pl.BlockSpec((3, bb, 3 * W), lambda i: (0, i, 0))]
    out_shape = [jax.ShapeDtypeStruct(ymix.shape, ymix.dtype),
                 jax.ShapeDtypeStruct(s0_all.shape, F32),
                 jax.ShapeDtypeStruct((3, n_seq, 3 * W), F32)]
    kern = functools.partial(_delta_decode_kernel, n_heads=H, bb=bb, has_prev=snew_all is not None)
    head_scr = pltpu.VMEM((H, bb, HEAD_DIM), F32)
    return pl.pallas_call(
        kern,
        grid=(n_seq // bb,),
        in_specs=in_specs,
        out_specs=out_specs,
        out_shape=out_shape,
        scratch_shapes=[head_scr, head_scr, head_scr, head_scr, pltpu.VMEM((bb, LANES), F32)],
        input_output_aliases=aliases,
        compiler_params=_cparams(1),
        name="delta_decode",
    )(*args)


def _s5_prep_kernel(are_ref, aim_ref, ls_ref, bre_ref, bim_ref,
                    lbre_ref, lbim_ref, bbre_ref, bbim_ref, *, n_state, group):
    a_re = are_ref[...]
    a_im = aim_ref[...]
    dt = jnp.exp(ls_ref[...])
    mag = jnp.exp(a_re * dt)
    ang = a_im * dt
    lb_re = mag * jnp.cos(ang)
    lb_im = mag * jnp.sin(ang)
    den = a_re * a_re + a_im * a_im
    num_re = lb_re - 1.0
    num_im = lb_im
    f_re = (num_re * a_re + num_im * a_im) / den
    f_im = (num_im * a_re - num_re * a_im) / den
    lbre_ref[...] = lb_re
    lbim_ref[...] = lb_im
    r = lax.broadcasted_iota(jnp.int32, (n_state, n_state * group), 0)
    cidx = lax.broadcasted_iota(jnp.int32, (n_state, n_state * group), 1)
    expand = jnp.where(_idiv_pow2(cidx, group) == r, 1.0, 0.0).astype(F32)
    fre_x = _dot(f_re, expand, HIGHEST)
    fim_x = _dot(f_im, expand, HIGHEST)
    b_re = bre_ref[...]
    b_im = bim_ref[...]
    bbre_ref[...] = fre_x * b_re - fim_x * b_im
    bbim_ref[...] = fre_x * b_im + fim_x * b_re


def s5_prep(a_re, a_im, log_step, b_re, b_im):
    g, n = a_re.shape
    group = b_re.shape[-1]
    full = lambda shape: pl.BlockSpec(shape, lambda: (0,) * len(shape))
    kern = functools.partial(_s5_prep_kernel, n_state=n, group=group)
    lb_re, lb_im, bb_re, bb_im = pl.pallas_call(
        kern,
        in_specs=[full((g, n)), full((g, n)), full((g, 1)), full((g, n * group)), full((g, n * group))],
        out_specs=[full((g, n)), full((g, n)), full((g, n * group)), full((g, n * group))],
        out_shape=[jax.ShapeDtypeStruct((g, n), F32), jax.ShapeDtypeStruct((g, n), F32),
                   jax.ShapeDtypeStruct((g, n * group), F32), jax.ShapeDtypeStruct((g, n * group), F32)],
        compiler_params=pltpu.CompilerParams(vmem_limit_bytes=VMEM_LIMIT_BYTES),
        name="s5_prep",
    )(a_re, a_im, log_step.reshape(g, 1), b_re.reshape(g, n * group), b_im.reshape(g, n * group))
    return lb_re, lb_im, bb_re.reshape(g, n, group), bb_im.reshape(g, n, group)


def _s5_decode_kernel(u_ref, h0r_ref, h0i_ref, lbr_ref, lbi_ref, wbr_ref, wbi_ref, wc_ref,
                      d_ref, wg_ref, bg_ref, ymix_any,
                      y_ref, hr_out, hi_out,
                      xr_scr, xi_scr, yc_scr, *, n_blocks, blk_in, blk_state):
    del ymix_any
    u = u_ref[...]
    u16 = u.astype(BF16)
    for j in range(n_blocks):
        uj = u16[:, j * blk_in:(j + 1) * blk_in]
        xr_scr[:, j * blk_state:(j + 1) * blk_state] = _dot(uj, wbr_ref[j])
        xi_scr[:, j * blk_state:(j + 1) * blk_state] = _dot(uj, wbi_ref[j])

    n_lanes = n_blocks * blk_state
    lane_chunk = _pick_tile(n_lanes, 1024, LANES)
    for lc in range(n_lanes // lane_chunk):
        ls = slice(lc * lane_chunk, (lc + 1) * lane_chunk)
        lbr, lbi = lbr_ref[:, ls], lbi_ref[:, ls]
        hr, hi = h0r_ref[:, ls], h0i_ref[:, ls]
        xr = lbr * hr - lbi * hi + xr_scr[:, ls]
        xi = lbr * hi + lbi * hr + xi_scr[:, ls]
        xr_scr[:, ls] = xr
        xi_scr[:, ls] = xi
        hr_out[:, ls] = xr
        hi_out[:, ls] = xi

    for j in range(n_blocks):
        ss = slice(j * blk_state, (j + 1) * blk_state)
        yj = _dot(xr_scr[:, ss].astype(BF16), wc_ref[j, :blk_state, :])
        yj = yj + _dot(xi_scr[:, ss].astype(BF16), wc_ref[j, blk_state:, :])
        yc_scr[:, j * blk_in:(j + 1) * blk_in] = yj
    y = _gelu_tanh(yc_scr[...] + d_ref[...] * u)
    gate = _dot(y.astype(BF16), wg_ref[...]) + bg_ref[...]
    y_ref[...] = (y * _sigmoid(gate)).astype(y_ref.dtype)


def s5_mixer_decode(uba, h0_re, h0_im, lb_re, lb_im, wb_re, wb_im, wc, d, w_glu16, b_glu, ymix, *,
                    row_off, n_groups, rb, width, col_block):
    n_blocks, blk_in, blk_state = wb_re.shape
    n_lanes = n_blocks * blk_state
    assert row_off % rb == 0 and rb % (2 * SUBLANES) == 0
    ro = row_off // rb

    def const(shape):
        return pl.BlockSpec(shape, lambda s: (0,) * len(shape))

    state_spec = pl.BlockSpec((rb, n_lanes), lambda s: (s, 0))
    in_specs = [pl.BlockSpec((rb, width), lambda s: (ro + s, 0)),
                state_spec, state_spec,
                const((1, n_lanes)), const((1, n_lanes)),
                const(wb_re.shape), const(wb_im.shape), const(wc.shape),
                const((1, width)), const(w_glu16.shape), const((1, width)),
                pl.BlockSpec(memory_space=pl.ANY)]
    out_specs = [pl.BlockSpec((rb, width), lambda s: (ro + s, col_block)), state_spec, state_spec]
    out_shape = [jax.ShapeDtypeStruct(ymix.shape, ymix.dtype),
                 jax.ShapeDtypeStruct(h0_re.shape, F32),
                 jax.ShapeDtypeStruct(h0_im.shape, F32)]
    kern = functools.partial(_s5_decode_kernel, n_blocks=n_blocks, blk_in=blk_in, blk_state=blk_state)
    return pl.pallas_call(
        kern,
        grid=(n_groups,),
        in_specs=in_specs,
        out_specs=out_specs,
        out_shape=out_shape,
        scratch_shapes=[pltpu.VMEM((rb, n_lanes), F32), pltpu.VMEM((rb, n_lanes), F32),
                        pltpu.VMEM((rb, width), F32)],
        input_output_aliases={11: 0},
        compiler_params=_cparams(1),
        name="s5_mixer_decode",
    )(uba, h0_re, h0_im, lb_re.reshape(1, n_lanes), lb_im.reshape(1, n_lanes),
      wb_re, wb_im, wc, d.reshape(1, width), w_glu16, b_glu.reshape(1, width), ymix)


def _s5_prompt_kernel(u_ref, h0r_ref, h0i_ref, lbr_ref, lbi_ref, wbr_ref, wbi_ref, wc_ref,
                      d_ref, wg_ref, bg_ref, ymix_any,
                      y_ref, hr_out, hi_out,
                      xr_scr, xi_scr, hr_scr, hi_scr, yc_scr, *, rb, n_blocks, blk_in, blk_state):
    del ymix_any
    t_blk = pl.program_id(0)
    nt = pl.num_programs(0)
    R = SUBLANES
    half = R // 2
    ng = rb // R
    nb2 = n_blocks // 2
    n_fold = nb2 * blk_state

    @pl.when(t_blk == 0)
    def _init():
        hr_scr[...] = h0r_ref[...]
        hi_scr[...] = h0i_ref[...]

    u = u_ref[...]
    low_in = lax.broadcasted_iota(jnp.int32, (ng, R, blk_in), 1) < half
    for j in range(nb2):
        ua = u[:, j * blk_in:(j + 1) * blk_in].reshape(ng, R, blk_in)
        ub = u[:, (j + nb2) * blk_in:(j + nb2 + 1) * blk_in].reshape(ng, R, blk_in)
        ra = pltpu.roll(ua, half, 1)
        rbk = pltpu.roll(ub, half, 1)
        a_even, a_odd = jnp.where(low_in, ua, 0.0), jnp.where(low_in, ra, 0.0)
        b_even, b_odd = jnp.where(low_in, 0.0, rbk), jnp.where(low_in, 0.0, ub)
        lhs_a = jnp.stack([a_even, a_odd], axis=1).reshape(2 * ng, R, blk_in)
        lhs_b = jnp.stack([b_even, b_odd], axis=1).reshape(2 * ng, R, blk_in)
        lhs = _b16(jnp.concatenate([lhs_a, lhs_b], axis=-1).reshape(2 * rb, 2 * blk_in))
        ss = slice(j * blk_state, (j + 1) * blk_state)
        xr_scr[:, :, ss] = _dot(lhs, wbr_ref[j]).reshape(2 * ng, R, blk_state)
        xi_scr[:, :, ss] = _dot(lhs, wbi_ref[j]).reshape(2 * ng, R, blk_state)

    lane_chunk = _pick_tile(n_fold, 1024, LANES)
    for lc in range(n_fold // lane_chunk):
        ls = slice(lc * lane_chunk, (lc + 1) * lane_chunk)
        low = lax.broadcasted_iota(jnp.int32, (R, lane_chunk), 0) < half
        l0 = slice(lc * lane_chunk, (lc + 1) * lane_chunk)
        l1 = slice(n_fold + lc * lane_chunk, n_fold + (lc + 1) * lane_chunk)
        lbr = jnp.where(low, lbr_ref[:, l0], lbr_ref[:, l1])
        lbi = jnp.where(low, lbi_ref[:, l0], lbi_ref[:, l1])

        def step(t, carry):
            hr, hi = carry
            xr = lbr * hr - lbi * hi + xr_scr[t, :, ls]
            xi = lbr * hi + lbi * hr + xi_scr[t, :, ls]
            xr_scr[t, :, ls] = xr
            xi_scr[t, :, ls] = xi
            return xr, xi

        hr, hi = lax.fori_loop(0, 2 * ng, step, (hr_scr[:, ls], hi_scr[:, ls]))
        hr_scr[:, ls] = hr
        hi_scr[:, ls] = hi

    @pl.when(t_blk == nt - 1)
    def _fin():
        hr_out[...] = hr_scr[...]
        hi_out[...] = hi_scr[...]

    low_st = lax.broadcasted_iota(jnp.int32, (ng, R, blk_state), 1) < half

    def unfold(x):
        x4 = x.reshape(ng, 2, R, blk_state)
        even, odd = x4[:, 0], x4[:, 1]
        f0 = jnp.where(low_st, even, pltpu.roll(odd, half, 1)).reshape(rb, blk_state)
        f1 = jnp.where(low_st, pltpu.roll(even, half, 1), odd).reshape(rb, blk_state)
        return _b16(f0), _b16(f1)

    for j in range(nb2):
        ss = slice(j * blk_state, (j + 1) * blk_state)
        r0, r1 = unfold(xr_scr[:, :, ss])
        i0, i1 = unfold(xi_scr[:, :, ss])
        j1 = j + nb2
        yc_scr[:, j * blk_in:(j + 1) * blk_in] = (
            _dot(r0, wc_ref[j, :blk_state, :]) + _dot(i0, wc_ref[j, blk_state:, :]))
        yc_scr[:, j1 * blk_in:(j1 + 1) * blk_in] = (
            _dot(r1, wc_ref[j1, :blk_state, :]) + _dot(i1, wc_ref[j1, blk_state:, :]))
    y = _gelu_tanh(yc_scr[...] + d_ref[...] * u)
    gate = _dot(y.astype(BF16), wg_ref[...]) + bg_ref[...]
    y_ref[...] = (y * _sigmoid(gate)).astype(y_ref.dtype)


def s5_mixer_prompt(uba, h0_re, h0_im, lb_re, lb_im, wb2_re, wb2_im, wc, d, w_glu16, b_glu, ymix, *,
                    n_time_blocks, rb, width, col_block):
    nb2, blk_in2, blk_state = wb2_re.shape
    blk_in = blk_in2 // 2
    n_blocks = 2 * nb2
    n_lanes = n_blocks * blk_state
    n_fold = n_lanes // 2
    assert rb % (2 * SUBLANES) == 0

    def const(shape):
        return pl.BlockSpec(shape, lambda t: (0,) * len(shape), pipeline_mode=pl.Buffered(1))

    state_spec = pl.BlockSpec((SUBLANES, n_fold), lambda t: (0, 0))
    in_specs = [pl.BlockSpec((rb, width), lambda t: (t, 0)),
                state_spec, state_spec,
                const((1, n_lanes)), const((1, n_lanes)),
                const(wb2_re.shape), const(wb2_im.shape), const(wc.shape),
                const((1, width)), const(w_glu16.shape), const((1, width)),
                pl.BlockSpec(memory_space=pl.ANY)]
    out_specs = [pl.BlockSpec((rb, width), lambda t: (t, col_block)), state_spec, state_spec]
    out_shape = [jax.ShapeDtypeStruct(ymix.shape, ymix.dtype),
                 jax.ShapeDtypeStruct(h0_re.shape, F32),
                 jax.ShapeDtypeStruct(h0_im.shape, F32)]
    kern = functools.partial(_s5_prompt_kernel, rb=rb, n_blocks=n_blocks, blk_in=blk_in,
                             blk_state=blk_state)
    scan_scr = pltpu.VMEM((2 * rb // SUBLANES, SUBLANES, n_fold), F32)
    return pl.pallas_call(
        kern,
        grid=(n_time_blocks,),
        in_specs=in_specs,
        out_specs=out_specs,
        out_shape=out_shape,
        scratch_shapes=[scan_scr, scan_scr,
                        pltpu.VMEM((SUBLANES, n_fold), F32), pltpu.VMEM((SUBLANES, n_fold), F32),
                        pltpu.VMEM((rb, width), F32)],
        input_output_aliases={11: 0},
        compiler_params=_cparams(1),
        name="s5_mixer_prompt",
    )(uba, h0_re, h0_im, lb_re.reshape(1, n_lanes), lb_im.reshape(1, n_lanes),
      wb2_re, wb2_im, wc, d.reshape(1, width), w_glu16, b_glu.reshape(1, width), ymix)


def _s5_block_weights(bb_re, bb_im, c_re, c_im):
    g, n, grp = bb_re.shape
    r = S5_GROUPS_PER_BLOCK
    nb = g // r
    eye = jnp.eye(r, dtype=F32)

    def in_blocks(bb):
        w = jnp.einsum("jsnc,rs->jrcsn", bb.reshape(nb, r, n, grp), eye)
        return w.reshape(nb, r * grp, r * n).astype(BF16)

    def out_blocks(cc):
        w = jnp.einsum("jrcn,rs->jsnrc", cc.reshape(nb, r, grp, n), eye)
        return w.reshape(nb, r * n, r * grp)

    wc = jnp.concatenate([out_blocks(c_re), -out_blocks(c_im)], axis=1).astype(BF16)
    return in_blocks(bb_re), in_blocks(bb_im), wc


def kernel(x_prompt, x_sample, state_delta, state_conv, state_s5_re, state_s5_im, norm_mix, w_in, conv_w, delta_a_log, delta_dt_bias, delta_norm, s5_a_re, s5_a_im, s5_log_step, s5_b_re, s5_b_im, s5_c_re, s5_c_im, s5_d, s5_w_glu, s5_b_glu, w_out, norm_ffn, w_ffn_in, w_ffn_out, norm_final):
    B, L, D = x_prompt.shape
    Bs = x_sample.shape[0]
    depth = w_in.shape[0]
    H = delta_a_log.shape[1]
    Wd = H * HEAD_DIM
    G, N = s5_a_re.shape[1:]
    grp = s5_b_re.shape[-1]
    Ws = G * grp
    d_ff = w_ffn_out.shape[1]
    Mp = B * L
    M = Mp + Bs
    half = SUBLANES // 2
    assert B == half and x_sample.shape[1] == 1 and Wd % Ws == 0
    uba_w = Ws + GATE_TILES * LANES

    x = to_rows(x_prompt, x_sample.reshape(Bs, D))

    gate_cols = 4 * Wd
    w_qkvz16 = w_in[:, :, :gate_cols].astype(BF16)
    w_uba = jnp.concatenate([w_in[:, :, gate_cols + 2 * H:], w_in[:, :, gate_cols:gate_cols + 2 * H],
                             jnp.zeros((depth, D, GATE_TILES * LANES - 2 * H), F32)], axis=2).astype(BF16)
    pad_h = lambda a: jnp.pad(a, (0, LANES - H)).reshape(1, LANES)
    pad_h2 = lambda a: jnp.pad(a, (H, LANES - 2 * H)).reshape(1, LANES)
    zero_s = jnp.zeros((B, H, HEAD_DIM, HEAD_DIM), F32)
    zero_c = jnp.zeros((B, 3, 3 * Wd), F32)
    zero_h = jnp.zeros((SUBLANES, G * N // 2), F32)
    s5_rb = _pick_tile(Mp, 256, 2 * SUBLANES)
    s5_rb_dec = _pick_tile(Bs, 32, 2 * SUBLANES)

    w_ffn_out16 = cast_bf16(w_ffn_out)
    s_dec = None
    outs = {k: [] for k in ("dp", "cp", "rp", "ip", "cs", "rs", "is")}
    for l in range(depth):
        hmix = rmsnorm_rows(x, norm_mix[l], BF16)
        qkvz = matmul(hmix, w_qkvz16, l, n=gate_cols, tn=1024)
        uba = matmul(hmix, w_uba, l, n=uba_w, tn=1280)

        dnorm = delta_norm[l].reshape(1, HEAD_DIM)
        ymix, s_p, cnq, cnk, cnv = delta_prompt(
            qkvz, uba, conv_w[l], zero_c, zero_s, pad_h(delta_a_log[l]), pad_h(delta_dt_bias[l]),
            dnorm, seq=L, batch=B, n_heads=H, ymix_shape=(M, Wd + Ws))
        ymix, s_dec, cn_s = delta_decode(
            qkvz, uba, conv_w[l], jnp.transpose(state_conv[l], (1, 0, 2)), state_delta, l,
            pad_h2(delta_a_log[l]), pad_h2(delta_dt_bias[l]), dnorm, ymix, s_dec,
            row_off=Mp, n_seq=Bs, n_heads=H)
        outs["dp"].append(s_p)
        outs["cp"].append(jnp.concatenate([cnq, cnk, cnv], axis=-1))
        outs["cs"].append(jnp.transpose(cn_s, (1, 0, 2)))

        lb_re, lb_im, bb_re, bb_im = s5_prep(s5_a_re[l], s5_a_im[l], s5_log_step[l],
                                             s5_b_re[l], s5_b_im[l])
        wb_re, wb_im, wc = _s5_block_weights(bb_re, bb_im, s5_c_re[l], s5_c_im[l])
        w_glu16 = s5_w_glu[l].astype(BF16)
        nb2 = wb_re.shape[0] // 2
        ymix, hr_p, hi_p = s5_mixer_prompt(
            uba, zero_h, zero_h, lb_re, lb_im,
            jnp.concatenate([wb_re[:nb2], wb_re[nb2:]], axis=1),
            jnp.concatenate([wb_im[:nb2], wb_im[nb2:]], axis=1),
            wc, s5_d[l], w_glu16, s5_b_glu[l], ymix,
            n_time_blocks=Mp // s5_rb, rb=s5_rb, width=Ws, col_block=Wd // Ws)
        ymix, hr_s, hi_s = s5_mixer_decode(
            uba, state_s5_re[l].reshape(Bs, G * N), state_s5_im[l].reshape(Bs, G * N),
            lb_re, lb_im, wb_re, wb_im, wc, s5_d[l], w_glu16, s5_b_glu[l], ymix,
            row_off=Mp, n_groups=Bs // s5_rb_dec, rb=s5_rb_dec, width=Ws, col_block=Wd // Ws)
        unfold = lambda a: jnp.transpose(a.reshape(2, B, G * N // 2), (1, 0, 2)).reshape(B, G, N)
        outs["rp"].append(unfold(hr_p))
        outs["ip"].append(unfold(hi_p))
        outs["rs"].append(hr_s.reshape(Bs, G, N))
        outs["is"].append(hi_s.reshape(Bs, G, N))

        x = matmul(ymix, w_out, l, n=D, res=x)

        hffn = rmsnorm_rows(x, norm_ffn[l], BF16)
        hidden = matmul(hffn, w_ffn_in, l, n=d_ff, col_off2=d_ff, out_dtype=BF16, tm=2080, tn=256,
                        rows_outer=True, x_single_buffer=True)
        x = matmul(hidden, w_ffn_out16, l, n=D, res=x, tm=520, tn=512, rows_outer=True)

    y_prompt, y_sample = final_norm(x, norm_final, batch=B, seq=L, n_decode=Bs)
    y_sample = y_sample.reshape(Bs, 1, D)
    st = lambda k: jnp.stack(outs[k])
    return (y_prompt, y_sample, st("dp"), st("cp"), st("rp"), st("ip"),
            s_dec, st("cs"), st("rs"), st("is"))
```

```python
import functools

import jax
import jax.numpy as jnp
from jax import lax
from jax.experimental import pallas as pl
from jax.experimental.pallas import tpu as pltpu

F32 = jnp.float32
BF16 = jnp.bfloat16
EPS = 1e-6

LANES = 128
SUBLANES = 8
VMEM_LIMIT_BYTES = 58 * 1024 * 1024
HEAD_DIM = 128
DELTA_CHUNK = 128
SOLVE_BLOCK = 16
S5_GROUPS_PER_BLOCK = 8
GATE_TILES = 4
HIGHEST = lax.Precision.HIGHEST


def _cparams(n_axes):
    return pltpu.CompilerParams(dimension_semantics=("arbitrary",) * n_axes,
                                vmem_limit_bytes=VMEM_LIMIT_BYTES)


def _pick_tile(n, target, mult):
    best = None
    for d in range(mult, min(n, target) + 1, mult):
        if n % d == 0:
            best = d
    return n if best is None else best


def _idiv_pow2(x, c):
    shift = c.bit_length() - 1
    assert c == 1 << shift
    return lax.shift_right_logical(x, jnp.int32(shift))


def _sigmoid(x):
    return 0.5 * jnp.tanh(0.5 * x) + 0.5


def _silu(x):
    h = 0.5 * x
    return h * jnp.tanh(h) + h


def _softplus(x):
    return jnp.maximum(x, 0.0) + jnp.log(1.0 + jnp.exp(-jnp.abs(x)))


def _gelu_tanh(x):
    c = 0.7978845608028654
    return 0.5 * x * (1.0 + jnp.tanh(c * (x + 0.044715 * (x * x * x))))


def _dot(a, b, precision=None):
    return jnp.dot(a, b, preferred_element_type=F32, precision=precision)


def _dot_nt(a, b, precision=None):
    return lax.dot_general(a, b, (((1,), (1,)), ((), ())), preferred_element_type=F32,
                           precision=precision)


def _dot_tn(a, b, precision=None):
    return lax.dot_general(a, b, (((0,), (0,)), ((), ())), preferred_element_type=F32,
                           precision=precision)


def _b16(x):
    return x.astype(BF16)


def _rmsnorm_kernel(x_ref, w_ref, o_ref):
    x = x_ref[...]
    y = x * lax.rsqrt(jnp.mean(x * x, axis=-1, keepdims=True) + EPS)
    o_ref[...] = (y * w_ref[...]).astype(o_ref.dtype)


def rmsnorm_rows(x, w, out_dtype):
    m, d = x.shape
    tr = _pick_tile(m, 832, 16)
    return pl.pallas_call(
        _rmsnorm_kernel,
        grid=(m // tr,),
        in_specs=[pl.BlockSpec((tr, d), lambda i: (i, 0)),
                  pl.BlockSpec((1, d), lambda i: (0, 0))],
        out_specs=pl.BlockSpec((tr, d), lambda i: (i, 0)),
        out_shape=jax.ShapeDtypeStruct((m, d), out_dtype),
        compiler_params=_cparams(1),
        name="rmsnorm_rows",
    )(x, w.reshape(1, d))


def _interleave4(xs):
    t, d = xs[0].shape
    ng = t // SUBLANES
    sub = lax.broadcasted_iota(jnp.int32, (ng, SUBLANES, d), 1)
    x3 = [x.reshape(ng, SUBLANES, d) for x in xs]
    outs = []
    for q in range(4):
        acc = jnp.zeros((ng, SUBLANES, d), xs[0].dtype)
        for b in range(4):
            first = pltpu.roll(x3[b], (b - 2 * q) % SUBLANES, 1)
            second = pltpu.roll(x3[b], (3 + b - 2 * q) % SUBLANES, 1)
            acc = jnp.where(sub == b, first, jnp.where(sub == 4 + b, second, acc))
        outs.append(acc)
    return jnp.stack(outs, axis=1).reshape(4 * t, d)


def _deinterleave4(y):
    rows, d = y.shape
    t = rows // 4
    ng = t // SUBLANES
    sub = lax.broadcasted_iota(jnp.int32, (ng, SUBLANES, d), 1)
    y4 = y.reshape(ng, 4, SUBLANES, d)
    outs = []
    for b in range(4):
        acc = jnp.zeros((ng, SUBLANES, d), y.dtype)
        for q in range(4):
            yq = y4[:, q]
            first = pltpu.roll(yq, (2 * q - b) % SUBLANES, 1)
            second = pltpu.roll(yq, (2 * q - 3 - b) % SUBLANES, 1)
            acc = jnp.where(sub == 2 * q, first, jnp.where(sub == 2 * q + 1, second, acc))
        outs.append(acc.reshape(t, d))
    return outs


def _to_rows_kernel(xp_ref, xs_ref, o_ref, *, n_prompt_blocks):
    i = pl.program_id(0)

    @pl.when(i < n_prompt_blocks)
    def _prompt():
        o_ref[...] = _interleave4([xp_ref[b] for b in range(4)])

    @pl.when(i >= n_prompt_blocks)
    def _decode():
        o_ref[...] = xs_ref[...]


def to_rows(x_prompt, x_sample2d):
    b, l, d = x_prompt.shape
    bs = x_sample2d.shape[0]
    assert b == 4 and bs % (4 * SUBLANES) == 0 and (4 * l) % bs == 0
    tl = bs // 4
    npb = l // tl
    kern = functools.partial(_to_rows_kernel, n_prompt_blocks=npb)
    return pl.pallas_call(
        kern,
        grid=(npb + 1,),
        in_specs=[pl.BlockSpec((4, tl, d), lambda i: (0, jnp.minimum(i, npb - 1), 0)),
                  pl.BlockSpec((bs, d), lambda i: (0, 0))],
        out_specs=pl.BlockSpec((bs, d), lambda i: (i, 0)),
        out_shape=jax.ShapeDtypeStruct((4 * l + bs, d), x_prompt.dtype),
        compiler_params=_cparams(1),
        name="to_rows",
    )(x_prompt, x_sample2d)


def _final_norm_kernel(x_ref, w_ref, yp_ref, ys_ref, *, n_prompt_blocks):
    i = pl.program_id(0)
    x = x_ref[...]
    y = x * lax.rsqrt(jnp.mean(x * x, axis=-1, keepdims=True) + EPS) * w_ref[...]

    @pl.when(i < n_prompt_blocks)
    def _prompt():
        for b, yb in enumerate(_deinterleave4(y)):
            yp_ref[b] = yb

    @pl.when(i >= n_prompt_blocks)
    def _decode():
        ys_ref[...] = y


def final_norm(x, w, *, batch, seq, n_decode):
    m, d = x.shape
    bs = n_decode
    assert batch == 4 and m == 4 * seq + bs and bs % (4 * SUBLANES) == 0 and (4 * seq) % bs == 0
    tl = bs // 4
    npb = seq // tl
    kern = functools.partial(_final_norm_kernel, n_prompt_blocks=npb)
    return pl.pallas_call(
        kern,
        grid=(npb + 1,),
        in_specs=[pl.BlockSpec((bs, d), lambda i: (i, 0)),
                  pl.BlockSpec((1, d), lambda i: (0, 0))],
        out_specs=[pl.BlockSpec((4, tl, d), lambda i: (0, jnp.minimum(i, npb - 1), 0)),
                   pl.BlockSpec((bs, d), lambda i: (0, 0))],
        out_shape=[jax.ShapeDtypeStruct((4, seq, d), x.dtype), jax.ShapeDtypeStruct((bs, d), x.dtype)],
        compiler_params=_cparams(1),
        name="final_norm",
    )(x, w.reshape(1, d))


def _mm_kernel(*refs, n_w, has_res):
    x_ref = refs[0]
    w_refs = refs[1:1 + n_w]
    res_ref = refs[1 + n_w] if has_res else None
    o_ref = refs[-1]
    x = x_ref[...]
    parts = [_dot(x, _b16(w_ref[...])) for w_ref in w_refs]
    y = parts[0]
    if n_w == 2:
        y = _silu(y) * parts[1]
    if has_res:
        y = y + res_ref[...]
    o_ref[...] = y.astype(o_ref.dtype)


def matmul(x, w, layer, *, n, col_off=0, col_off2=None, res=None, out_dtype=F32,
           tm=1040, tn=512, rows_outer=False, x_single_buffer=False):
    m, k_total = x.shape
    assert w.shape[1] == k_total
    tm = _pick_tile(m, tm, 16)
    tn = _pick_tile(n, tn, LANES)
    assert col_off % tn == 0 and (col_off2 is None or col_off2 % tn == 0)
    n_w = 1 if col_off2 is None else 2
    if rows_outer:
        grid = (m // tm, n // tn)
        ij = lambda a, b: (a, b)
    else:
        grid = (n // tn, m // tm)
        ij = lambda a, b: (b, a)
    x_mode = dict(pipeline_mode=pl.Buffered(1)) if (x_single_buffer and rows_outer) else {}
    in_specs = [pl.BlockSpec((tm, k_total), lambda a, b: (ij(a, b)[0], 0), **x_mode)]
    args = [x]
    for off in (col_off, col_off2)[:n_w]:
        in_specs.append(pl.BlockSpec((None, k_total, tn),
                                     lambda a, b, o=off // tn: (layer, 0, ij(a, b)[1] + o)))
        args.append(w)
    if res is not None:
        in_specs.append(pl.BlockSpec((tm, tn), lambda a, b: ij(a, b)))
        args.append(res)
    kern = functools.partial(_mm_kernel, n_w=n_w, has_res=res is not None)
    return pl.pallas_call(
        kern,
        grid=grid,
        in_specs=in_specs,
        out_specs=pl.BlockSpec((tm, tn), lambda a, b: ij(a, b)),
        out_shape=jax.ShapeDtypeStruct((m, n), out_dtype),
        compiler_params=_cparams(2),
        name="matmul_rows",
    )(*args)


def _cast_kernel(x_ref, o_ref):
    o_ref[...] = x_ref[...].astype(o_ref.dtype)


def cast_bf16(w):
    depth, k, n = w.shape
    tr = _pick_tile(k, 512, 16)
    return pl.pallas_call(
        _cast_kernel,
        grid=(depth, k // tr),
        in_specs=[pl.BlockSpec((None, tr, n), lambda l, i: (l, i, 0))],
        out_specs=pl.BlockSpec((None, tr, n), lambda l, i: (l, i, 0)),
        out_shape=jax.ShapeDtypeStruct(w.shape, BF16),
        compiler_params=_cparams(2),
        name="cast_bf16",
    )(w)


def _lane_pick(x, idx):
    lane = lax.broadcasted_iota(jnp.int32, x.shape, 1)
    return jnp.sum(jnp.where(lane == idx, x, 0.0), axis=-1, keepdims=True)


def _unit_lower_inverse_minus_eye(a_list):
    C = DELTA_CHUNK
    assert C == 8 * SOLVE_BLOCK
    row = lax.broadcasted_iota(jnp.int32, (C, C), 0)
    col = lax.broadcasted_iota(jnp.int32, (C, C), 1)
    same_block = _idiv_pow2(row, SOLVE_BLOCK) == _idiv_pow2(col, SOLVE_BLOCK)
    d = [jnp.where(same_block, a, 0.0) for a in a_list]
    e = [a - dd for a, dd in zip(a_list, d)]
    p = [-dd for dd in d]
    td = list(p)
    for _ in range(3):
        p = [_dot(_b16(x), _b16(x)) for x in p]
        td = [t + x + _dot(_b16(t), _b16(x)) for t, x in zip(td, p)]
    f = [ee + _dot(_b16(t), _b16(ee)) for t, ee in zip(td, e)]
    f2 = [_dot(_b16(x), _b16(x)) for x in f]
    f4 = [_dot(_b16(x), _b16(x)) for x in f2]
    h1 = [x2 - x - _dot(_b16(x), _b16(x2)) for x, x2 in zip(f, f2)]
    gd = [h + x4 + _dot(_b16(h), _b16(x4)) for h, x4 in zip(h1, f4)]
    return [g + t + _dot(_b16(g), _b16(t)) for g, t in zip(gd, td)]


def _delta_prompt_kernel(q_ref, k_ref, v_ref, z_ref, ba_ref, cwq_ref, cwk_ref, cwv_ref,
                         c0q_ref, c0k_ref, c0v_ref, s0_ref, alog_ref, dtb_ref, dn_ref,
                         y_ref, sfin_ref, cnq_ref, cnk_ref, cnv_ref,
                         s_scr, xbuf, ybuf, *, n_heads, lc, batch):
    h = pl.program_id(0)
    c = pl.program_id(1)
    nc = pl.num_programs(1)
    C = DELTA_CHUNK
    B = batch
    n_chunks = lc // C
    PAD = SUBLANES

    def rows_of(ref, b):
        return ref[pl.ds(b, lc, stride=B), :]

    @pl.when(c == 0)
    def _init():
        s_scr[...] = s0_ref[...]
        for b in range(B):
            xbuf[0, b, PAD - 3:PAD, :] = c0q_ref[b]
            xbuf[1, b, PAD - 3:PAD, :] = c0k_ref[b]
            xbuf[2, b, PAD - 3:PAD, :] = c0v_ref[b]

    def conv_silu(j, b, x_ref, cw_ref, cn_ref):
        xbuf[j, b, PAD:PAD + lc, :] = rows_of(x_ref, b)
        cw = cw_ref[...]
        acc = xbuf[j, b, PAD - 3:PAD - 3 + lc, :] * cw[0:1, :]
        acc = acc + xbuf[j, b, PAD - 2:PAD - 2 + lc, :] * cw[1:2, :]
        acc = acc + xbuf[j, b, PAD - 1:PAD - 1 + lc, :] * cw[2:3, :]
        acc = acc + xbuf[j, b, PAD:PAD + lc, :] * cw[3:4, :]
        tail = xbuf[j, b, PAD + lc - 3:PAD + lc, :]
        xbuf[j, b, PAD - 3:PAD, :] = tail

        @pl.when(c == nc - 1)
        def _():
            cn_ref[b] = tail

        return _silu(acc)

    a_log = _lane_pick(alog_ref[...], h)
    dt_b = _lane_pick(dtb_ref[...], h)
    neg_a = -jnp.exp(a_log)

    crow = lax.broadcasted_iota(jnp.int32, (C, C), 0)
    ccol = lax.broadcasted_iota(jnp.int32, (C, C), 1)
    lane = lax.broadcasted_iota(jnp.int32, (C, LANES), 1)

    qs, ks, vs, betas, gmat = [], [], [], [], jnp.zeros((C, LANES), F32)
    for b in range(B):
        q = conv_silu(0, b, q_ref, cwq_ref, cnq_ref)
        k = conv_silu(1, b, k_ref, cwk_ref, cnk_ref)
        v = conv_silu(2, b, v_ref, cwv_ref, cnv_ref)
        q = q * (lax.rsqrt(jnp.sum(q * q, axis=-1, keepdims=True) + EPS) * (HEAD_DIM ** -0.5))
        k = k * lax.rsqrt(jnp.sum(k * k, axis=-1, keepdims=True) + EPS)
        ba = rows_of(ba_ref, b)
        beta = _sigmoid(_lane_pick(ba, h))
        g = neg_a * _softplus(_lane_pick(ba, h + n_heads) + dt_b)
        for ci in range(n_chunks):
            sl = slice(ci * C, (ci + 1) * C)
            u = b * n_chunks + ci
            qs.append(q[sl]); ks.append(k[sl]); vs.append(v[sl]); betas.append(beta[sl])
            gmat = gmat + jnp.where(lane == u, g[sl], 0.0)
    n_units = B * n_chunks
    assert n_units <= LANES

    tri = jnp.where(crow >= ccol, 1.0, 0.0).astype(F32)
    gc_mat = _dot(tri, gmat, HIGHEST)
    gc_t = jnp.transpose(gc_mat)

    lower_incl = crow >= ccol
    lower_strict = crow > ccol

    def chunk_local(us):
        pick = lambda xs: [xs[u] for u in us]
        q_, k_, v_, b_ = pick(qs), pick(ks), pick(vs), pick(betas)
        gcs = [gc_mat[:, u:u + 1] for u in us]
        egc = [jnp.exp(x) for x in gcs]
        kb = [x * y for x, y in zip(k_, b_)]
        k16 = [_b16(x) for x in k_]
        kq = [_dot_nt(_b16(jnp.concatenate([x, y], axis=0)), z) for x, y, z in zip(kb, q_, k16)]
        decay_incl = [jnp.exp(jnp.where(lower_incl, g - gc_t[u:u + 1, :], -jnp.inf))
                      for g, u in zip(gcs, us)]
        a_strict = [jnp.where(lower_strict, x[:C] * d_, 0.0) for x, d_ in zip(kq, decay_incl)]
        attn16 = [_b16(x[C:] * d_) for x, d_ in zip(kq, decay_incl)]
        md = _unit_lower_inverse_minus_eye(a_strict)
        rhs = [jnp.concatenate([x * y, z * e], axis=-1)
               for x, y, z, e in zip(v_, b_, kb, egc)]
        sol = [r + _dot(_b16(m), _b16(r)) for m, r in zip(md, rhs)]
        gc_last = [x[C - 1:C, :] for x in gcs]
        kcum_qdec16 = [_b16(jnp.concatenate([s_[:, HEAD_DIM:], x * e], axis=0))
                       for s_, x, e in zip(sol, q_, egc)]
        kdec16 = [_b16(x * jnp.exp(gl - g)) for x, gl, g in zip(k_, gc_last, gcs)]
        g_last = [jnp.exp(x) for x in gc_last]
        value = [s_[:, :HEAD_DIM] for s_ in sol]
        return attn16, value, kcum_qdec16, kdec16, g_last

    attn16, value, kcum_qdec16, kdec16, g_last = chunk_local(list(range(n_units)))
    s = [s_scr[b] for b in range(B)]
    o_units = [None] * n_units
    for ci in range(n_chunks):
        us = [b * n_chunks + ci for b in range(B)]
        ks_qs = [_dot(kcum_qdec16[u], _b16(s[b])) for b, u in enumerate(us)]
        v_new16 = [_b16(value[u] - x[:C]) for u, x in zip(us, ks_qs)]
        for b, u in enumerate(us):
            o_units[u] = ks_qs[b][C:] + _dot(attn16[u], v_new16[b])
            s[b] = s[b] * g_last[u] + _dot_tn(kdec16[u], v_new16[b])
    for b in range(B):
        s_scr[b] = s[b]

    @pl.when(c == nc - 1)
    def _():
        for b in range(B):
            sfin_ref[b] = s[b]

    dn = dn_ref[...]
    for b in range(B):
        o = jnp.concatenate(o_units[b * n_chunks:(b + 1) * n_chunks], axis=0)
        o = o * lax.rsqrt(jnp.mean(o * o, axis=-1, keepdims=True) + EPS) * dn
        ybuf[pl.ds(b, lc, stride=B), :] = o * _silu(rows_of(z_ref, b))
    y_ref[...] = ybuf[...].astype(y_ref.dtype)


def delta_prompt(qkvz, uba, conv_w, conv0, s0, alog_pad, dtb_pad, d_norm, *,
                 seq, batch, n_heads, ymix_shape):
    H = n_heads
    B = batch
    lc = _pick_tile(seq, 512, DELTA_CHUNK)
    assert lc % DELTA_CHUNK == 0
    nblk = seq // lc
    gate_block = uba.shape[1] // LANES - GATE_TILES
    rb = lc * B

    def col_spec(part):
        return pl.BlockSpec((rb, LANES), lambda h, c: (c, part * H + h))

    def cw_spec(part):
        return pl.BlockSpec((4, LANES), lambda h, c: (0, part * H + h))

    def c0_spec(part):
        return pl.BlockSpec((B, 3, LANES), lambda h, c: (0, 0, part * H + h))

    pad_spec = pl.BlockSpec((1, LANES), lambda h, c: (0, 0))
    state_spec = pl.BlockSpec((B, None, HEAD_DIM, HEAD_DIM), lambda h, c: (0, h, 0, 0))
    in_specs = [col_spec(0), col_spec(1), col_spec(2), col_spec(3),
                pl.BlockSpec((rb, LANES), lambda h, c: (c, gate_block)),
                cw_spec(0), cw_spec(1), cw_spec(2),
                c0_spec(0), c0_spec(1), c0_spec(2),
                state_spec, pad_spec, pad_spec, pad_spec]
    cn_spec = pl.BlockSpec((B, 3, LANES), lambda h, c: (0, 0, h))
    out_specs = [pl.BlockSpec((rb, LANES), lambda h, c: (c, h)), state_spec, cn_spec, cn_spec, cn_spec]
    cn_shape = jax.ShapeDtypeStruct((B, 3, H * HEAD_DIM), F32)
    out_shape = [jax.ShapeDtypeStruct(ymix_shape, BF16),
                 jax.ShapeDtypeStruct((B, H, HEAD_DIM, HEAD_DIM), F32),
                 cn_shape, cn_shape, cn_shape]
    kern = functools.partial(_delta_prompt_kernel, n_heads=H, lc=lc, batch=B)
    return pl.pallas_call(
        kern,
        grid=(H, nblk),
        in_specs=in_specs,
        out_specs=out_specs,
        out_shape=out_shape,
        scratch_shapes=[pltpu.VMEM((B, HEAD_DIM, HEAD_DIM), F32),
                        pltpu.VMEM((3, B, SUBLANES + lc, LANES), F32),
                        pltpu.VMEM((rb, LANES), F32)],
        compiler_params=_cparams(2),
        name="delta_prompt",
    )(qkvz, qkvz, qkvz, qkvz, uba, conv_w, conv_w, conv_w,
      conv0, conv0, conv0, s0, alog_pad, dtb_pad, d_norm)


def _delta_decode_kernel(q_ref, k_ref, v_ref, z_ref, ba_ref, cw_ref, c0_ref, s0_ref,
                         alog_ref, dtb_ref, dn_ref, *rest, n_heads, bb, has_prev):
    n_any = 2 if has_prev else 1
    y_ref, snew_ref, cnew_ref, qn_scr, kn_scr, vn_scr, o_scr, bg_scr = rest[n_any:]
    H = n_heads
    W = H * HEAD_DIM
    cw = cw_ref[...]

    def conv_silu(part, x_ref):
        x = x_ref[...]
        sl = slice(part * W, (part + 1) * W)
        acc = c0_ref[0, :, sl] * cw[0:1, sl]
        acc = acc + c0_ref[1, :, sl] * cw[1:2, sl]
        acc = acc + c0_ref[2, :, sl] * cw[2:3, sl]
        acc = acc + x * cw[3:4, sl]
        cnew_ref[0, :, sl] = c0_ref[1, :, sl]
        cnew_ref[1, :, sl] = c0_ref[2, :, sl]
        cnew_ref[2, :, sl] = x
        return _silu(acc)

    q = conv_silu(0, q_ref)
    k = conv_silu(1, k_ref)
    v = conv_silu(2, v_ref)
    for h in range(H):
        hs = slice(h * HEAD_DIM, (h + 1) * HEAD_DIM)
        qh, kh = q[:, hs], k[:, hs]
        qn_scr[h] = qh * lax.rsqrt(jnp.sum(qh * qh, axis=-1, keepdims=True) + EPS) * (HEAD_DIM ** -0.5)
        kn_scr[h] = kh * lax.rsqrt(jnp.sum(kh * kh, axis=-1, keepdims=True) + EPS)
        vn_scr[h] = v[:, hs]

    ba = ba_ref[...]
    lane = lax.broadcasted_iota(jnp.int32, ba.shape, 1)
    beta_all = _sigmoid(ba)
    g_all = -jnp.exp(alog_ref[...]) * _softplus(ba + dtb_ref[...])
    bg_scr[...] = jnp.where(lane < H, beta_all, g_all)

    ROWS16 = 2 * SUBLANES

    def split(x):
        hi = _b16(x).astype(F32)
        return hi, _b16(x - hi).astype(F32)

    def stack16(rows):
        pad = jnp.zeros((ROWS16 - len(rows), HEAD_DIM), F32)
        return _b16(jnp.concatenate(list(rows) + [pad], axis=0))

    def per_seq(bi, carry):
        bg = bg_scr[pl.ds(bi, 1), :]
        heads = range(H)
        k_row = [kn_scr[h, pl.ds(bi, 1), :] for h in heads]
        q_row = [qn_scr[h, pl.ds(bi, 1), :] for h in heads]
        v_row = [vn_scr[h, pl.ds(bi, 1), :] for h in heads]
        beta = [bg[:, h:h + 1] for h in heads]
        eg = [jnp.exp(bg[:, H + h:H + h + 1]) for h in heads]
        s = [s0_ref[bi, h] for h in heads]
        k_sp = [split(x) for x in k_row]
        q_sp = [split(x) for x in q_row]
        s_sp = [split(x) for x in s]
        lhs = [stack16([kh, kl, qh, ql]) for (kh, kl), (qh, ql) in zip(k_sp, q_sp)]
        r_hi = [_dot(l_, _b16(sh)) for l_, (sh, _) in zip(lhs, s_sp)]
        r_lo = [_dot(l_, _b16(sl)) for l_, (_, sl) in zip(lhs, s_sp)]
        ks = [a[0:1] + a[1:2] + b[0:1] for a, b in zip(r_hi, r_lo)]
        qs = [a[2:3] + a[3:4] + b[2:3] for a, b in zip(r_hi, r_lo)]
        v_new = [b_ * v_ - (b_ * e_) * ks_ for b_, v_, e_, ks_ in zip(beta, v_row, eg, ks)]
        qk = [jnp.sum(q_ * k_, axis=-1, keepdims=True) for q_, k_ in zip(q_row, k_row)]
        o = [e_ * qs_ + qk_ * vn_ for e_, qs_, qk_, vn_ in zip(eg, qs, qk, v_new)]
        vn_sp = [split(x) for x in v_new]
        outer = [_dot_tn(stack16([kh, kh, kl]), stack16([vh, vl, vh]))
                 for (kh, kl), (vh, vl) in zip(k_sp, vn_sp)]
        for h in heads:
            snew_ref[bi, h] = s[h] * eg[h] + outer[h]
            o_scr[h, pl.ds(bi, 1), :] = o[h]
        return carry

    lax.fori_loop(0, bb, per_seq, 0)

    dn = dn_ref[...]
    z = z_ref[...]
    for h in range(H):
        hs = slice(h * HEAD_DIM, (h + 1) * HEAD_DIM)
        o = o_scr[h]
        o = o * lax.rsqrt(jnp.mean(o * o, axis=-1, keepdims=True) + EPS) * dn
        y_ref[:, hs] = (o * _silu(z[:, hs])).astype(y_ref.dtype)


def delta_decode(qkvz, uba, conv_w, conv0_t, s0_all, layer, alog_pad2, dtb_pad2, d_norm, ymix,
                 snew_all, *, row_off, n_seq, n_heads):
    H = n_heads
    W = H * HEAD_DIM
    bb = SUBLANES
    assert n_seq % bb == 0 and row_off % bb == 0
    ro = row_off // bb
    gate_block = uba.shape[1] // LANES - GATE_TILES

    def col_spec(part):
        return pl.BlockSpec((bb, W), lambda i: (ro + i, part))

    pad_spec = pl.BlockSpec((1, LANES), lambda i: (0, 0))
    state_spec = pl.BlockSpec((None, bb, H, HEAD_DIM, HEAD_DIM), lambda i: (layer, i, 0, 0, 0))
    in_specs = [col_spec(0), col_spec(1), col_spec(2), col_spec(3),
                pl.BlockSpec((bb, LANES), lambda i: (ro + i, gate_block)),
                pl.BlockSpec((4, 3 * W), lambda i: (0, 0)),
                pl.BlockSpec((3, bb, 3 * W), lambda i: (0, i, 0)),
                state_spec,
                pad_spec, pad_spec, pad_spec,
                pl.BlockSpec(memory_space=pl.ANY)]
    args = [qkvz, qkvz, qkvz, qkvz, uba, conv_w, conv0_t, s0_all, alog_pad2, dtb_pad2, d_norm, ymix]
    aliases = {11: 0}
    if snew_all is not None:
        in_specs.append(pl.BlockSpec(memory_space=pl.ANY))
        args.append(snew_all)
        aliases[12] = 1
    out_specs = [pl.BlockSpec((bb, W), lambda i: (ro + i, 0)),
                 state_spec,
                 pl.BlockSpec((3, bb, 3 * W), lambda i: (0, i, 0))]
    out_shape = [jax.ShapeDtypeStruct(ymix.shape, ymix.dtype),
                 jax.ShapeDtypeStruct(s0_all.shape, F32),
                 jax.ShapeDtypeStruct((3, n_seq, 3 * W), F32)]
    kern = functools.partial(_delta_decode_kernel, n_heads=H, bb=bb, has_prev=snew_all is not None)
    head_scr = pltpu.VMEM((H, bb, HEAD_DIM), F32)
    return pl.pallas_call(
        kern,
        grid=(n_seq // bb,),
        in_specs=in_specs,
        out_specs=out_specs,
        out_shape=out_shape,
        scratch_shapes=[head_scr, head_scr, head_scr, head_scr, pltpu.VMEM((bb, LANES), F32)],
        input_output_aliases=aliases,
        compiler_params=_cparams(1),
        name="delta_decode",
    )(*args)


def _s5_prep_kernel(are_ref, aim_ref, ls_ref, bre_ref, bim_ref,
                    lbre_ref, lbim_ref, bbre_ref, bbim_ref, *, n_state, group):
    a_re = are_ref[...]
    a_im = aim_ref[...]
    dt = jnp.exp(ls_ref[...])
    mag = jnp.exp(a_re * dt)
    ang = a_im * dt
    lb_re = mag * jnp.cos(ang)
    lb_im = mag * jnp.sin(ang)
    den = a_re * a_re + a_im * a_im
    num_re = lb_re - 1.0
    num_im = lb_im
    f_re = (num_re * a_re + num_im * a_im) / den
    f_im = (num_im * a_re - num_re * a_im) / den
    lbre_ref[...] = lb_re
    lbim_ref[...] = lb_im
    r = lax.broadcasted_iota(jnp.int32, (n_state, n_state * group), 0)
    cidx = lax.broadcasted_iota(jnp.int32, (n_state, n_state * group), 1)
    expand = jnp.where(_idiv_pow2(cidx, group) == r, 1.0, 0.0).astype(F32)
    fre_x = _dot(f_re, expand, HIGHEST)
    fim_x = _dot(f_im, expand, HIGHEST)
    b_re = bre_ref[...]
    b_im = bim_ref[...]
    bbre_ref[...] = fre_x * b_re - fim_x * b_im
    bbim_ref[...] = fre_x * b_im + fim_x * b_re


def s5_prep(a_re, a_im, log_step, b_re, b_im):
    g, n = a_re.shape
    group = b_re.shape[-1]
    full = lambda shape: pl.BlockSpec(shape, lambda: (0,) * len(shape))
    kern = functools.partial(_s5_prep_kernel, n_state=n, group=group)
    lb_re, lb_im, bb_re, bb_im = pl.pallas_call(
        kern,
        in_specs=[full((g, n)), full((g, n)), full((g, 1)), full((g, n * group)), full((g, n * group))],
        out_specs=[full((g, n)), full((g, n)), full((g, n * group)), full((g, n * group))],
        out_shape=[jax.ShapeDtypeStruct((g, n), F32), jax.ShapeDtypeStruct((g, n), F32),
                   jax.ShapeDtypeStruct((g, n * group), F32), jax.ShapeDtypeStruct((g, n * group), F32)],
        compiler_params=pltpu.CompilerParams(vmem_limit_bytes=VMEM_LIMIT_BYTES),
        name="s5_prep",
    )(a_re, a_im, log_step.reshape(g, 1), b_re.reshape(g, n * group), b_im.reshape(g, n * group))
    return lb_re, lb_im, bb_re.reshape(g, n, group), bb_im.reshape(g, n, group)


def _s5_decode_kernel(u_ref, h0r_ref, h0i_ref, lbr_ref, lbi_ref, wbr_ref, wbi_ref, wc_ref,
                      d_ref, wg_ref, bg_ref, ymix_any,
                      y_ref, hr_out, hi_out,
                      xr_scr, xi_scr, yc_scr, *, n_blocks, blk_in, blk_state):
    del ymix_any
    u = u_ref[...]
    u16 = u.astype(BF16)
    for j in range(n_blocks):
        uj = u16[:, j * blk_in:(j + 1) * blk_in]
        xr_scr[:, j * blk_state:(j + 1) * blk_state] = _dot(uj, wbr_ref[j])
        xi_scr[:, j * blk_state:(j + 1) * blk_state] = _dot(uj, wbi_ref[j])

    n_lanes = n_blocks * blk_state
    lane_chunk = _pick_tile(n_lanes, 1024, LANES)
    for lc in range(n_lanes // lane_chunk):
        ls = slice(lc * lane_chunk, (lc + 1) * lane_chunk)
        lbr, lbi = lbr_ref[:, ls], lbi_ref[:, ls]
        hr, hi = h0r_ref[:, ls], h0i_ref[:, ls]
        xr = lbr * hr - lbi * hi + xr_scr[:, ls]
        xi = lbr * hi + lbi * hr + xi_scr[:, ls]
        xr_scr[:, ls] = xr
        xi_scr[:, ls] = xi
        hr_out[:, ls] = xr
        hi_out[:, ls] = xi

    for j in range(n_blocks):
        ss = slice(j * blk_state, (j + 1) * blk_state)
        yj = _dot(xr_scr[:, ss].astype(BF16), wc_ref[j, :blk_state, :])
        yj = yj + _dot(xi_scr[:, ss].astype(BF16), wc_ref[j, blk_state:, :])
        yc_scr[:, j * blk_in:(j + 1) * blk_in] = yj
    y = _gelu_tanh(yc_scr[...] + d_ref[...] * u)
    gate = _dot(y.astype(BF16), wg_ref[...]) + bg_ref[...]
    y_ref[...] = (y * _sigmoid(gate)).astype(y_ref.dtype)


def s5_mixer_decode(uba, h0_re, h0_im, lb_re, lb_im, wb_re, wb_im, wc, d, w_glu16, b_glu, ymix, *,
                    row_off, n_groups, rb, width, col_block):
    n_blocks, blk_in, blk_state = wb_re.shape
    n_lanes = n_blocks * blk_state
    assert row_off % rb == 0 and rb % (2 * SUBLANES) == 0
    ro = row_off // rb

    def const(shape):
        return pl.BlockSpec(shape, lambda s: (0,) * len(shape))

    state_spec = pl.BlockSpec((rb, n_lanes), lambda s: (s, 0))
    in_specs = [pl.BlockSpec((rb, width), lambda s: (ro + s, 0)),
                state_spec, state_spec,
                const((1, n_lanes)), const((1, n_lanes)),
                const(wb_re.shape), const(wb_im.shape), const(wc.shape),
                const((1, width)), const(w_glu16.shape), const((1, width)),
                pl.BlockSpec(memory_space=pl.ANY)]
    out_specs = [pl.BlockSpec((rb, width), lambda s: (ro + s, col_block)), state_spec, state_spec]
    out_shape = [jax.ShapeDtypeStruct(ymix.shape, ymix.dtype),
                 jax.ShapeDtypeStruct(h0_re.shape, F32),
                 jax.ShapeDtypeStruct(h0_im.shape, F32)]
    kern = functools.partial(_s5_decode_kernel, n_blocks=n_blocks, blk_in=blk_in, blk_state=blk_state)
    return pl.pallas_call(
        kern,
        grid=(n_groups,),
        in_specs=in_specs,
        out_specs=out_specs,
        out_shape=out_shape,
        scratch_shapes=[pltpu.VMEM((rb, n_lanes), F32), pltpu.VMEM((rb, n_lanes), F32),
                        pltpu.VMEM((rb, width), F32)],
        input_output_aliases={11: 0},
        compiler_params=_cparams(1),
        name="s5_mixer_decode",
    )(uba, h0_re, h0_im, lb_re.reshape(1, n_lanes), lb_im.reshape(1, n_lanes),
      wb_re, wb_im, wc, d.reshape(1, width), w_glu16, b_glu.reshape(1, width), ymix)


def _s5_prompt_kernel(u_ref, h0r_ref, h0i_ref, lbr_ref, lbi_ref, wbr_ref, wbi_ref, wc_ref,
                      d_ref, wg_ref, bg_ref, ymix_any,
                      y_ref, hr_out, hi_out,
                      xr_scr, xi_scr, hr_scr, hi_scr, yc_scr, *, rb, n_blocks, blk_in, blk_state):
    del ymix_any
    t_blk = pl.program_id(0)
    nt = pl.num_programs(0)
    R = SUBLANES
    half = R // 2
    ng = rb // R
    nb2 = n_blocks // 2
    n_fold = nb2 * blk_state

    @pl.when(t_blk == 0)
    def _init():
        hr_scr[...] = h0r_ref[...]
        hi_scr[...] = h0i_ref[...]

    u = u_ref[...]
    low_in = lax.broadcasted_iota(jnp.int32, (ng, R, blk_in), 1) < half
    for j in range(nb2):
        ua = u[:, j * blk_in:(j + 1) * blk_in].reshape(ng, R, blk_in)
        ub = u[:, (j + nb2) * blk_in:(j + nb2 + 1) * blk_in].reshape(ng, R, blk_in)
        ra = pltpu.roll(ua, half, 1)
        rbk = pltpu.roll(ub, half, 1)
        a_even, a_odd = jnp.where(low_in, ua, 0.0), jnp.where(low_in, ra, 0.0)
        b_even, b_odd = jnp.where(low_in, 0.0, rbk), jnp.where(low_in, 0.0, ub)
        lhs_a = jnp.stack([a_even, a_odd], axis=1).reshape(2 * ng, R, blk_in)
        lhs_b = jnp.stack([b_even, b_odd], axis=1).reshape(2 * ng, R, blk_in)
        lhs = _b16(jnp.concatenate([lhs_a, lhs_b], axis=-1).reshape(2 * rb, 2 * blk_in))
        ss = slice(j * blk_state, (j + 1) * blk_state)
        xr_scr[:, :, ss] = _dot(lhs, wbr_ref[j]).reshape(2 * ng, R, blk_state)
        xi_scr[:, :, ss] = _dot(lhs, wbi_ref[j]).reshape(2 * ng, R, blk_state)

    lane_chunk = _pick_tile(n_fold, 1024, LANES)
    for lc in range(n_fold // lane_chunk):
        ls = slice(lc * lane_chunk, (lc + 1) * lane_chunk)
        low = lax.broadcasted_iota(jnp.int32, (R, lane_chunk), 0) < half
        l0 = slice(lc * lane_chunk, (lc + 1) * lane_chunk)
        l1 = slice(n_fold + lc * lane_chunk, n_fold + (lc + 1) * lane_chunk)
        lbr = jnp.where(low, lbr_ref[:, l0], lbr_ref[:, l1])
        lbi = jnp.where(low, lbi_ref[:, l0], lbi_ref[:, l1])

        def step(t, carry):
            hr, hi = carry
            xr = lbr * hr - lbi * hi + xr_scr[t, :, ls]
            xi = lbr * hi + lbi * hr + xi_scr[t, :, ls]
            xr_scr[t, :, ls] = xr
            xi_scr[t, :, ls] = xi
            return xr, xi

        hr, hi = lax.fori_loop(0, 2 * ng, step, (hr_scr[:, ls], hi_scr[:, ls]))
        hr_scr[:, ls] = hr
        hi_scr[:, ls] = hi

    @pl.when(t_blk == nt - 1)
    def _fin():
        hr_out[...] = hr_scr[...]
        hi_out[...] = hi_scr[...]

    low_st = lax.broadcasted_iota(jnp.int32, (ng, R, blk_state), 1) < half

    def unfold(x):
        x4 = x.reshape(ng, 2, R, blk_state)
        even, odd = x4[:, 0], x4[:, 1]
        f0 = jnp.where(low_st, even, pltpu.roll(odd, half, 1)).reshape(rb, blk_state)
        f1 = jnp.where(low_st, pltpu.roll(even, half, 1), odd).reshape(rb, blk_state)
        return _b16(f0), _b16(f1)

    for j in range(nb2):
        ss = slice(j * blk_state, (j + 1) * blk_state)
        r0, r1 = unfold(xr_scr[:, :, ss])
        i0, i1 = unfold(xi_scr[:, :, ss])
        j1 = j + nb2
        yc_scr[:, j * blk_in:(j + 1) * blk_in] = (
            _dot(r0, wc_ref[j, :blk_state, :]) + _dot(i0, wc_ref[j, blk_state:, :]))
        yc_scr[:, j1 * blk_in:(j1 + 1) * blk_in] = (
            _dot(r1, wc_ref[j1, :blk_state, :]) + _dot(i1, wc_ref[j1, blk_state:, :]))
    y = _gelu_tanh(yc_scr[...] + d_ref[...] * u)
    gate = _dot(y.astype(BF16), wg_ref[...]) + bg_ref[...]
    y_ref[...] = (y * _sigmoid(gate)).astype(y_ref.dtype)


def s5_mixer_prompt(uba, h0_re, h0_im, lb_re, lb_im, wb2_re, wb2_im, wc, d, w_glu16, b_glu, ymix, *,
                    n_time_blocks, rb, width, col_block):
    nb2, blk_in2, blk_state = wb2_re.shape
    blk_in = blk_in2 // 2
    n_blocks = 2 * nb2
    n_lanes = n_blocks * blk_state
    n_fold = n_lanes // 2
    assert rb % (2 * SUBLANES) == 0

    def const(shape):
        return pl.BlockSpec(shape, lambda t: (0,) * len(shape), pipeline_mode=pl.Buffered(1))

    state_spec = pl.BlockSpec((SUBLANES, n_fold), lambda t: (0, 0))
    in_specs = [pl.BlockSpec((rb, width), lambda t: (t, 0)),
                state_spec, state_spec,
                const((1, n_lanes)), const((1, n_lanes)),
                const(wb2_re.shape), const(wb2_im.shape), const(wc.shape),
                const((1, width)), const(w_glu16.shape), const((1, width)),
                pl.BlockSpec(memory_space=pl.ANY)]
    out_specs = [pl.BlockSpec((rb, width), lambda t: (t, col_block)), state_spec, state_spec]
    out_shape = [jax.ShapeDtypeStruct(ymix.shape, ymix.dtype),
                 jax.ShapeDtypeStruct(h0_re.shape, F32),
                 jax.ShapeDtypeStruct(h0_im.shape, F32)]
    kern = functools.partial(_s5_prompt_kernel, rb=rb, n_blocks=n_blocks, blk_in=blk_in,
                             blk_state=blk_state)
    scan_scr = pltpu.VMEM((2 * rb // SUBLANES, SUBLANES, n_fold), F32)
    return pl.pallas_call(
        kern,
        grid=(n_time_blocks,),
        in_specs=in_specs,
        out_specs=out_specs,
        out_shape=out_shape,
        scratch_shapes=[scan_scr, scan_scr,
                        pltpu.VMEM((SUBLANES, n_fold), F32), pltpu.VMEM((SUBLANES, n_fold), F32),
                        pltpu.VMEM((rb, width), F32)],
        input_output_aliases={11: 0},
        compiler_params=_cparams(1),
        name="s5_mixer_prompt",
    )(uba, h0_re, h0_im, lb_re.reshape(1, n_lanes), lb_im.reshape(1, n_lanes),
      wb2_re, wb2_im, wc, d.reshape(1, width), w_glu16, b_glu.reshape(1, width), ymix)


def _s5_block_weights(bb_re, bb_im, c_re, c_im):
    g, n, grp = bb_re.shape
    r = S5_GROUPS_PER_BLOCK
    nb = g // r
    eye = jnp.eye(r, dtype=F32)

    def in_blocks(bb):
        w = jnp.einsum("jsnc,rs->jrcsn", bb.reshape(nb, r, n, grp), eye)
        return w.reshape(nb, r * grp, r * n).astype(BF16)

    def out_blocks(cc):
        w = jnp.einsum("jrcn,rs->jsnrc", cc.reshape(nb, r, grp, n), eye)
        return w.reshape(nb, r * n, r * grp)

    wc = jnp.concatenate([out_blocks(c_re), -out_blocks(c_im)], axis=1).astype(BF16)
    return in_blocks(bb_re), in_blocks(bb_im), wc


def kernel(x_prompt, x_sample, state_delta, state_conv, state_s5_re, state_s5_im, norm_mix, w_in, conv_w, delta_a_log, delta_dt_bias, delta_norm, s5_a_re, s5_a_im, s5_log_step, s5_b_re, s5_b_im, s5_c_re, s5_c_im, s5_d, s5_w_glu, s5_b_glu, w_out, norm_ffn, w_ffn_in, w_ffn_out, norm_final):
    B, L, D = x_prompt.shape
    Bs = x_sample.shape[0]
    depth = w_in.shape[0]
    H = delta_a_log.shape[1]
    Wd = H * HEAD_DIM
    G, N = s5_a_re.shape[1:]
    grp = s5_b_re.shape[-1]
    Ws = G * grp
    d_ff = w_ffn_out.shape[1]
    Mp = B * L
    M = Mp + Bs
    half = SUBLANES // 2
    assert B == half and x_sample.shape[1] == 1 and Wd % Ws == 0
    uba_w = Ws + GATE_TILES * LANES

    x = to_rows(x_prompt, x_sample.reshape(Bs, D))

    gate_cols = 4 * Wd
    w_qkvz16 = w_in[:, :, :gate_cols].astype(BF16)
    w_uba = jnp.concatenate([w_in[:, :, gate_cols + 2 * H:], w_in[:, :, gate_cols:gate_cols + 2 * H],
                             jnp.zeros((depth, D, GATE_TILES * LANES - 2 * H), F32)], axis=2).astype(BF16)
    pad_h = lambda a: jnp.pad(a, (0, LANES - H)).reshape(1, LANES)
    pad_h2 = lambda a: jnp.pad(a, (H, LANES - 2 * H)).reshape(1, LANES)
    zero_s = jnp.zeros((B, H, HEAD_DIM, HEAD_DIM), F32)
    zero_c = jnp.zeros((B, 3, 3 * Wd), F32)
    zero_h = jnp.zeros((SUBLANES, G * N // 2), F32)
    s5_rb = _pick_tile(Mp, 256, 2 * SUBLANES)
    s5_rb_dec = _pick_tile(Bs, 32, 2 * SUBLANES)

    w_ffn_out16 = cast_bf16(w_ffn_out)
    w_out16 = cast_bf16(w_out)
    s_dec = None
    outs = {k: [] for k in ("dp", "cp", "rp", "ip", "cs", "rs", "is")}
    for l in range(depth):
        hmix = rmsnorm_rows(x, norm_mix[l], BF16)
        qkvz = matmul(hmix, w_qkvz16, l, n=gate_cols, tn=1024)
        uba = matmul(hmix, w_uba, l, n=uba_w, tn=1280)

        dnorm = delta_norm[l].reshape(1, HEAD_DIM)
        ymix, s_p, cnq, cnk, cnv = delta_prompt(
            qkvz, uba, conv_w[l], zero_c, zero_s, pad_h(delta_a_log[l]), pad_h(delta_dt_bias[l]),
            dnorm, seq=L, batch=B, n_heads=H, ymix_shape=(M, Wd + Ws))
        ymix, s_dec, cn_s = delta_decode(
            qkvz, uba, conv_w[l], jnp.transpose(state_conv[l], (1, 0, 2)), state_delta, l,
            pad_h2(delta_a_log[l]), pad_h2(delta_dt_bias[l]), dnorm, ymix, s_dec,
            row_off=Mp, n_seq=Bs, n_heads=H)
        outs["dp"].append(s_p)
        outs["cp"].append(jnp.concatenate([cnq, cnk, cnv], axis=-1))
        outs["cs"].append(jnp.transpose(cn_s, (1, 0, 2)))

        lb_re, lb_im, bb_re, bb_im = s5_prep(s5_a_re[l], s5_a_im[l], s5_log_step[l],
                                             s5_b_re[l], s5_b_im[l])
        wb_re, wb_im, wc = _s5_block_weights(bb_re, bb_im, s5_c_re[l], s5_c_im[l])
        w_glu16 = s5_w_glu[l].astype(BF16)
        nb2 = wb_re.shape[0] // 2
        ymix, hr_p, hi_p = s5_mixer_prompt(
            uba, zero_h, zero_h, lb_re, lb_im,
            jnp.concatenate([wb_re[:nb2], wb_re[nb2:]], axis=1),
            jnp.concatenate([wb_im[:nb2], wb_im[nb2:]], axis=1),
            wc, s5_d[l], w_glu16, s5_b_glu[l], ymix,
            n_time_blocks=Mp // s5_rb, rb=s5_rb, width=Ws, col_block=Wd // Ws)
        ymix, hr_s, hi_s = s5_mixer_decode(
            uba, state_s5_re[l].reshape(Bs, G * N), state_s5_im[l].reshape(Bs, G * N),
            lb_re, lb_im, wb_re, wb_im, wc, s5_d[l], w_glu16, s5_b_glu[l], ymix,
            row_off=Mp, n_groups=Bs // s5_rb_dec, rb=s5_rb_dec, width=Ws, col_block=Wd // Ws)
        unfold = lambda a: jnp.transpose(a.reshape(2, B, G * N // 2), (1, 0, 2)).reshape(B, G, N)
        outs["rp"].append(unfold(hr_p))
        outs["ip"].append(unfold(hi_p))
        outs["rs"].append(hr_s.reshape(Bs, G, N))
        outs["is"].append(hi_s.reshape(Bs, G, N))

        x = matmul(ymix, w_out16, l, n=D, res=x, tn=1024)

        hffn = rmsnorm_rows(x, norm_ffn[l], BF16)
        hidden = matmul(hffn, w_ffn_in, l, n=d_ff, col_off2=d_ff, out_dtype=BF16, tm=2080, tn=256,
                        rows_outer=True, x_single_buffer=True)
        x = matmul(hidden, w_ffn_out16, l, n=D, res=x, tm=520, tn=512, rows_outer=True)

    y_prompt, y_sample = final_norm(x, norm_final, batch=B, seq=L, n_decode=Bs)
    y_sample = y_sample.reshape(Bs, 1, D)
    st = lambda k: jnp.stack(outs[k])
    return (y_prompt, y_sample, st("dp"), st("cp"), st("rp"), st("ip"),
            s_dec, st("cs"), st("rs"), st("is"))
```
